```python
import math
import jax, jax.numpy as jnp
from jax import lax
import numpy as np

D_MODEL = 1024
BATCH = 8
SEQ = 4096
DEPTH = 2

MIX_WIDTH = D_MODEL
A_WIDTH = MIX_WIDTH // 2
A_GROUPS = 4
A_GROUP_DIM = A_WIDTH // A_GROUPS
CHUNK = 128
B_WIDTH = MIX_WIDTH - A_WIDTH
HEAD_DIM = 64
B_HEADS = B_WIDTH // HEAD_DIM
ROT_DIM = HEAD_DIM // 4
ROPE_THETA = 500000.0
DILATED_PATTERNS = ((128, 1), (512, 4), (2048, 16))
IN_COLS = 2 * A_WIDTH + 3 * B_WIDTH
D_FF = 4 * D_MODEL
CONV_WIDTH = 3
EPS = 1e-6
NEG_INF = -1e30

kernel_name = 'hybrid_gmlp_dilated_attn_convffn'


def rmsnorm(x, g):
    xf = x.astype(jnp.float32)
    y = xf * lax.rsqrt(jnp.mean(xf * xf, axis=-1, keepdims=True) + EPS)
    return (y * g.astype(jnp.float32)).astype(x.dtype)


def layernorm(x, g, b):
    xf = x.astype(jnp.float32)
    mu = jnp.mean(xf, axis=-1, keepdims=True)
    xc = xf - mu
    y = xc * lax.rsqrt(jnp.mean(xc * xc, axis=-1, keepdims=True) + EPS)
    return (y * g.astype(jnp.float32) + b.astype(jnp.float32)).astype(x.dtype)


def partial_rope(x):
    s = x.shape[1]
    half = ROT_DIM // 2
    inv = ROPE_THETA ** (-jnp.arange(0, ROT_DIM, 2, dtype=jnp.float32) / ROT_DIM)
    ang = jnp.arange(s, dtype=jnp.float32)[:, None] * inv[None, :]
    cos = jnp.cos(ang)[None, :, None, :]
    sin = jnp.sin(ang)[None, :, None, :]
    xf = x.astype(jnp.float32)
    x1, x2 = xf[..., :half], xf[..., half:ROT_DIM]
    out = jnp.concatenate([x1 * cos - x2 * sin, x2 * cos + x1 * sin, xf[..., ROT_DIM:]], axis=-1)
    return out.astype(x.dtype)


def dilated_branch(q, k, v, window, dilation):
    bsz, s, h, dh = q.shape
    band = window // dilation
    n = s // dilation
    nb = -(-n // band)
    pad = nb * band - n

    def to_blocks(t):
        t = t.reshape(bsz, n, dilation, h, dh).transpose(0, 2, 1, 3, 4)
        t = jnp.pad(t, ((0, 0), (0, 0), (0, pad), (0, 0), (0, 0)))
        return t.reshape(bsz, dilation, nb, band, h, dh)

    def with_prev(t):
        prev = jnp.pad(t[:, :, :-1], ((0, 0), (0, 0), (1, 0), (0, 0), (0, 0), (0, 0)))
        return jnp.concatenate([prev, t], axis=3)

    qb = to_blocks(q)
    kc = with_prev(to_blocks(k))
    vc = with_prev(to_blocks(v))
    scores = jnp.einsum('brnqhd,brnkhd->brnhqk', qb, kc).astype(jnp.float32) * (dh ** -0.5)
    qi = jnp.arange(band)[:, None]
    kj = jnp.arange(2 * band)[None, :]
    dist = qi + band - kj
    blk = jnp.arange(nb)[:, None, None]
    valid = (dist >= 0) & (dist <= band) & (blk * band + kj - band >= 0)
    scores = jnp.where(valid[None, None, :, None], scores, NEG_INF)
    m = jnp.max(scores, axis=-1, keepdims=True)
    p = jnp.exp(scores - m)
    l = jnp.sum(p, axis=-1, keepdims=True)
    o = jnp.einsum('brnhqk,brnkhd->brnqhd', p.astype(v.dtype), vc).astype(jnp.float32)
    l_t = l[..., 0].transpose(0, 1, 2, 4, 3)
    lse = (m[..., 0] + jnp.log(l[..., 0])).transpose(0, 1, 2, 4, 3)
    o = o / l_t[..., None]
    o = o.reshape(bsz, dilation, nb * band, h, dh)[:, :, :n]
    o = o.transpose(0, 2, 1, 3, 4).reshape(bsz, s, h, dh)
    lse = lse.reshape(bsz, dilation, nb * band, h)[:, :, :n]
    lse = lse.transpose(0, 2, 1, 3).reshape(bsz, s, h)
    return o, lse


def mixer_spatial_gating(za, v_norm_g, v_norm_b, w_spatial, b_spatial):
    bsz, s, _ = za.shape
    za = jax.nn.gelu(za, approximate=False)
    u, va = za[..., :A_WIDTH], za[..., A_WIDTH:]
    va = layernorm(va, v_norm_g, v_norm_b)
    vch = va.reshape(bsz, s // CHUNK, CHUNK, A_GROUPS, A_GROUP_DIM)
    ws = w_spatial * jnp.tril(jnp.ones((CHUNK, CHUNK), dtype=w_spatial.dtype))
    sg = jnp.einsum('gpq,bnqgc->bnpgc', ws, vch) + b_spatial.T[None, None, :, :, None]
    return u * sg.reshape(bsz, s, A_WIDTH)


def mixer_dilated_attention(zb):
    bsz, s, _ = zb.shape
    qkv = zb.reshape(bsz, s, 3, B_HEADS, HEAD_DIM)
    q = partial_rope(qkv[:, :, 0])
    k = partial_rope(qkv[:, :, 1])
    v = qkv[:, :, 2]
    outs, lses = zip(*[dilated_branch(q, k, v, w, d) for (w, d) in DILATED_PATTERNS])
    alpha = jax.nn.softmax(jnp.stack(lses, axis=0), axis=0)
    o = jnp.sum(alpha[..., None] * jnp.stack(outs, axis=0), axis=0)
    return o.reshape(bsz, s, B_WIDTH).astype(zb.dtype)


def conv_ffn(h, w_up, conv_w, conv_b, w_down):
    s = h.shape[1]
    up = jnp.einsum('bsd,df->bsf', h, w_up)
    up_pad = jnp.pad(up, ((0, 0), (CONV_WIDTH - 1, 0), (0, 0)))
    conv = conv_b + sum(conv_w[i] * up_pad[:, i:i + s] for i in range(CONV_WIDTH))
    gate, val = conv[..., :D_FF], conv[..., D_FF:]
    y = jax.nn.gelu(gate, approximate=True) * val
    return jnp.einsum('bsf,fd->bsd', y, w_down)


def setup_inputs(seed: int = 0) -> dict:
    key = jax.random.key(seed)
    ks = jax.random.split(key, 18)
    f32 = jnp.float32

    def nrm(k, shape, scale):
        return jax.random.normal(k, shape, f32) * scale

    def gain(k, shape):
        return 1.0 + 0.05 * jax.random.normal(k, shape, f32)

    L = DEPTH
    return {
        'x': jax.random.normal(ks[0], (BATCH, SEQ, D_MODEL), f32),
        'pre_mix_norm': gain(ks[1], (L, D_MODEL)),
        'w_in': nrm(ks[2], (L, D_MODEL, IN_COLS), D_MODEL ** -0.5),
        'v_norm_g': gain(ks[3], (L, A_WIDTH)),
        'v_norm_b': nrm(ks[4], (L, A_WIDTH), 0.02),
        'w_spatial': nrm(ks[5], (L, A_GROUPS, CHUNK, CHUNK), CHUNK ** -0.5),
        'b_spatial': gain(ks[6], (L, A_GROUPS, CHUNK)),
        'out_norm_a': gain(ks[7], (L, A_WIDTH)),
        'out_norm_b': gain(ks[8], (L, B_WIDTH)),
        'w_out': nrm(ks[9], (L, MIX_WIDTH, D_MODEL), MIX_WIDTH ** -0.5),
        'post_mix_norm': gain(ks[10], (L, D_MODEL)),
        'pre_ffn_norm': gain(ks[11], (L, D_MODEL)),
        'w_up': nrm(ks[12], (L, D_MODEL, 2 * D_FF), D_MODEL ** -0.5),
        'conv_w': nrm(ks[13], (L, CONV_WIDTH, 2 * D_FF), CONV_WIDTH ** -0.5),
        'conv_b': nrm(ks[14], (L, 2 * D_FF), 0.02),
        'w_down': nrm(ks[15], (L, D_FF, D_MODEL), D_FF ** -0.5),
        'post_ffn_norm': gain(ks[16], (L, D_MODEL)),
    }


def reference(x, pre_mix_norm, w_in, v_norm_g, v_norm_b, w_spatial, b_spatial,
              out_norm_a, out_norm_b, w_out, post_mix_norm, pre_ffn_norm,
              w_up, conv_w, conv_b, w_down, post_ffn_norm):
    for l in range(DEPTH):
        h = rmsnorm(x, pre_mix_norm[l])
        proj = jnp.einsum('bsd,de->bse', h, w_in[l])
        o_a = mixer_spatial_gating(proj[..., :2 * A_WIDTH], v_norm_g[l], v_norm_b[l],
                                   w_spatial[l], b_spatial[l])
        o_b = mixer_dilated_attention(proj[..., 2 * A_WIDTH:])
        mixed = jnp.concatenate([rmsnorm(o_a, out_norm_a[l]), rmsnorm(o_b, out_norm_b[l])], axis=-1)
        y = jnp.einsum('bse,ed->bsd', mixed, w_out[l])
        x = x + rmsnorm(y, post_mix_norm[l])
        h = rmsnorm(x, pre_ffn_norm[l])
        f = conv_ffn(h, w_up[l], conv_w[l], conv_b[l], w_down[l])
        x = x + rmsnorm(f, post_ffn_norm[l])
    return x
```

```python
import functools

import numpy as np
import jax
import jax.numpy as jnp
from jax import lax
from jax.experimental import pallas as pl
from jax.experimental.pallas import tpu as pltpu

F32 = jnp.float32
BF16 = jnp.bfloat16

A_GROUPS = 4
CHUNK = 128
HEAD_DIM = 64
ROT_DIM = 16
ROPE_THETA = 500000.0
DILATIONS = (1, 4, 16)
BAND = 128
CONV_WIDTH = 3
EPS = 1e-6
NEG_INF = -1e30

LANES = 128
SUBLANES = 8
VMEM_LIMIT_BYTES = 56 * 1024 * 1024

IN_TILE = 512
FFN_TILE = 512
FF_BLOCK = 512


def _rms(x, g):
    return x * lax.rsqrt(jnp.mean(x * x, axis=-1, keepdims=True) + EPS) * g


def _in_proj_kernel(x_ref, g_ref, w_ref, vg_ref, vb_ref, ws_ref, bs_ref, ga_ref,
                    cos_ref, sa_ref, sb_ref, oa_ref, q_ref, k_ref, v_ref):
    a_width = oa_ref.shape[-1]
    b_width = q_ref.shape[-1]
    tile = x_ref.shape[1]

    h = _rms(x_ref[0], g_ref[...]).astype(BF16)
    proj = jnp.dot(h, w_ref[...], preferred_element_type=F32)

    za = proj[:, :2 * a_width]
    za = 0.5 * za * (1.0 + lax.erf(za * np.float32(np.sqrt(0.5))))
    u, va = za[:, :a_width], za[:, a_width:]
    mu = jnp.mean(va, axis=-1, keepdims=True)
    vc = va - mu
    va = vc * lax.rsqrt(jnp.mean(vc * vc, axis=-1, keepdims=True) + EPS) * vg_ref[...] + vb_ref[...]
    va = va.astype(BF16)

    row = lax.broadcasted_iota(jnp.int32, (CHUNK, CHUNK), 0)
    col = lax.broadcasted_iota(jnp.int32, (CHUNK, CHUNK), 1)
    causal = col <= row
    cols = []
    for g in range(A_GROUPS):
        wsg = jnp.where(causal, ws_ref[g], 0.0).astype(BF16)
        bias = bs_ref[:, g:g + 1]
        lo = g * CHUNK
        rows = []
        for n in range(tile // CHUNK):
            vblk = va[n * CHUNK:(n + 1) * CHUNK, lo:lo + CHUNK]
            rows.append(jnp.dot(wsg, vblk, preferred_element_type=F32) + bias)
        cols.append(jnp.concatenate(rows, axis=0))
    o_a = u * jnp.concatenate(cols, axis=1)
    oa_ref[0] = _rms(o_a, ga_ref[...]).astype(oa_ref.dtype)

    cos, sa, sb = cos_ref[...], sa_ref[...], sb_ref[...]
    half = ROT_DIM // 2

    def rope(t):
        out = []
        for c in range(b_width // LANES):
            tc = t[:, c * LANES:(c + 1) * LANES]
            out.append(tc * cos + pltpu.roll(tc, LANES - half, axis=1) * sa
                       + pltpu.roll(tc, half, axis=1) * sb)
        return jnp.concatenate(out, axis=1)

    base = 2 * a_width
    q_ref[0] = rope(proj[:, base:base + b_width]) * np.float32(HEAD_DIM ** -0.5)
    k_ref[0] = rope(proj[:, base + b_width:base + 2 * b_width])
    v_ref[0] = proj[:, base + 2 * b_width:]


def _in_proj(x, g, w, vg, vb, ws, bs_t, ga, cos_t, sa_t, sb_t):
    bsz, seq, d_model = x.shape
    a_width = vg.shape[-1]
    b_width = (w.shape[1] - 2 * a_width) // 3
    tile = IN_TILE
    const = lambda *shape: pl.BlockSpec(shape, lambda b, t: (0,) * len(shape),
                                        pipeline_mode=pl.Buffered(1))
    tok = lambda width: pl.BlockSpec((1, tile, width), lambda b, t: (b, t, 0))
    pos = pl.BlockSpec((tile, LANES), lambda b, t: (t, 0))
    return pl.pallas_call(
        _in_proj_kernel,
        grid=(bsz, seq // tile),
        in_specs=[tok(d_model), const(1, d_model), const(*w.shape), const(1, a_width),
                  const(1, a_width), const(*ws.shape), const(*bs_t.shape), const(1, a_width),
                  pos, pos, pos],
        out_specs=[tok(a_width), tok(b_width), tok(b_width), tok(b_width)],
        out_shape=[jax.ShapeDtypeStruct((bsz, seq, a_width), BF16),
                   jax.ShapeDtypeStruct((bsz, seq, b_width), F32),
                   jax.ShapeDtypeStruct((bsz, seq, b_width), F32),
                   jax.ShapeDtypeStruct((bsz, seq, b_width), F32)],
        compiler_params=pltpu.CompilerParams(
            dimension_semantics=("parallel", "parallel"), vmem_limit_bytes=VMEM_LIMIT_BYTES),
        name="in_proj",
    )(x, g, w, vg, vb, ws, bs_t, ga, cos_t, sa_t, sb_t)


def _attn_kernel(q_ref, k_ref, v_ref, o_ref, acc_ref, m_ref, l_ref):
    seq = q_ref.shape[1]
    lane = lax.broadcasted_iota(jnp.int32, (BAND, LANES), 1)
    head0 = lane < HEAD_DIM
    qi = lax.broadcasted_iota(jnp.int32, (BAND, 2 * BAND), 0)
    kj = lax.broadcasted_iota(jnp.int32, (BAND, 2 * BAND), 1)
    dist = qi + BAND - kj
    in_band = (dist >= 0) & (dist <= BAND)
    cur_half = kj >= BAND

    def tile(rows_q, rows_prev, first):
        qb = q_ref[0, rows_q, :].astype(BF16)
        zero = jnp.zeros_like(qb)
        q2 = jnp.concatenate([jnp.where(head0, qb, zero), jnp.where(head0, zero, qb)], axis=0)
        kk = jnp.concatenate([k_ref[0, rows_prev, :], k_ref[0, rows_q, :]], axis=0).astype(BF16)
        vv = jnp.concatenate([v_ref[0, rows_prev, :], v_ref[0, rows_q, :]], axis=0).astype(BF16)
        s = lax.dot_general(q2, kk, (((1,), (1,)), ((), ())), preferred_element_type=F32)
        valid = in_band & (cur_half | jnp.logical_not(first))
        s = jnp.where(jnp.concatenate([valid, valid], axis=0), s, NEG_INF)
        m = jnp.max(s, axis=-1, keepdims=True)
        p = jnp.exp(s - m)
        l = jnp.sum(p, axis=-1, keepdims=True)
        o2 = jnp.dot(p.astype(BF16), vv, preferred_element_type=F32)
        acc = jnp.where(head0, o2[:BAND], o2[BAND:])
        mf = jnp.where(head0, jnp.broadcast_to(m[:BAND], (BAND, LANES)),
                       jnp.broadcast_to(m[BAND:], (BAND, LANES)))
        lf = jnp.where(head0, jnp.broadcast_to(l[:BAND], (BAND, LANES)),
                       jnp.broadcast_to(l[BAND:], (BAND, LANES)))
        return acc, mf, lf

    for slot, d in enumerate(DILATIONS[1:]):
        n_blk = seq // (d * BAND)

        def body(i, carry, slot=slot, d=d, n_blk=n_blk):
            r = i // n_blk
            blk = i % n_blk
            start = blk * (BAND * d) + r
            prev = jnp.maximum(start - BAND * d, r)
            rows_q = pl.ds(start, BAND, stride=d)
            rows_prev = pl.ds(prev, BAND, stride=d)
            acc, mf, lf = tile(rows_q, rows_prev, blk == 0)
            acc_ref[slot, rows_q, :] = acc
            m_ref[slot, rows_q, :] = mf
            l_ref[slot, rows_q, :] = lf
            return carry

        lax.fori_loop(0, d * n_blk, body, 0)

    def merge(n, carry):
        start = pl.multiple_of(n * BAND, BAND)
        prev = pl.multiple_of(jnp.maximum(n - 1, 0) * BAND, BAND)
        rows = pl.ds(start, BAND)
        acc1, m1, l1 = tile(rows, pl.ds(prev, BAND), n == 0)
        m2, m3 = m_ref[0, rows, :], m_ref[1, rows, :]
        m = jnp.maximum(jnp.maximum(m1, m2), m3)
        w1, w2, w3 = jnp.exp(m1 - m), jnp.exp(m2 - m), jnp.exp(m3 - m)
        num = w1 * acc1 + w2 * acc_ref[0, rows, :] + w3 * acc_ref[1, rows, :]
        den = w1 * l1 + w2 * l_ref[0, rows, :] + w3 * l_ref[1, rows, :]
        o_ref[0, rows, :] = num / den
        return carry

    lax.fori_loop(0, seq // BAND, merge, 0)


def _attn(q, k, v):
    bsz, seq, b_width = q.shape
    spec = pl.BlockSpec((1, seq, LANES), lambda b, c: (b, 0, c))
    n_dilated = len(DILATIONS) - 1
    return pl.pallas_call(
        _attn_kernel,
        grid=(bsz, b_width // LANES),
        in_specs=[spec, spec, spec],
        out_specs=spec,
        out_shape=jax.ShapeDtypeStruct((bsz, seq, b_width), F32),
        scratch_shapes=[pltpu.VMEM((n_dilated, seq, LANES), F32)] * 3,
        compiler_params=pltpu.CompilerParams(
            dimension_semantics=("parallel", "parallel"), vmem_limit_bytes=VMEM_LIMIT_BYTES),
        name="attn",
    )(q, k, v)


def _ffn_kernel(x_ref, oa_ref, ob_ref, gb_ref, wo_ref, gpm_ref, gpf_ref, wup_ref, cw_ref,
                cb_ref, wdn_ref, gpo_ref, out_ref, h_ref, up_ref, carry_ref, acc_ref,
                *, tiles_per_seq):
    tile = x_ref.shape[1]
    n_ff, ff_block = wdn_ref.shape[0], wdn_ref.shape[1]
    first_tile = pl.program_id(0) % tiles_per_seq == 0

    mixed = jnp.concatenate([oa_ref[0], _rms(ob_ref[0], gb_ref[...]).astype(BF16)], axis=1)
    y = jnp.dot(mixed, wo_ref[...], preferred_element_type=F32)
    x1 = x_ref[0] + _rms(y, gpm_ref[...])
    out_ref[0] = x1
    h_ref[...] = _rms(x1, gpf_ref[...]).astype(BF16)
    acc_ref[...] = jnp.zeros_like(acc_ref)

    def conv(part, j, up):
        prev = carry_ref[part, j]
        up_ref[part, pl.ds(0, SUBLANES), :] = jnp.where(first_tile, jnp.zeros_like(prev), prev)
        up_ref[part, pl.ds(SUBLANES, tile), :] = up
        carry_ref[part, j] = up[tile - SUBLANES:]
        w = cw_ref[part, j]
        out = cb_ref[part, j]
        for i in range(CONV_WIDTH):
            shift = SUBLANES - (CONV_WIDTH - 1) + i
            out = out + w[i:i + 1] * up_ref[part, pl.ds(shift, tile), :]
        return out

    def body(j, carry):
        h = h_ref[...]
        gate = conv(0, j, jnp.dot(h, wup_ref[0, j], preferred_element_type=F32))
        val = conv(1, j, jnp.dot(h, wup_ref[1, j], preferred_element_type=F32))
        cdf = 0.5 * (1.0 + jnp.tanh(np.float32(np.sqrt(2.0 / np.pi))
                                    * (gate + np.float32(0.044715) * (gate * gate * gate))))
        act = (gate * cdf * val).astype(BF16)
        acc_ref[...] += jnp.dot(act, wdn_ref[j], preferred_element_type=F32)
        return carry

    lax.fori_loop(0, n_ff, body, 0)
    out_ref[0] = out_ref[0] + _rms(acc_ref[...], gpo_ref[...])


def _ffn(x, oa, ob, gb, wo, gpm, gpf, wup, cw, cb, wdn, gpo):
    bsz, seq, d_model = x.shape
    tile = FFN_TILE
    tiles_per_seq = seq // tile
    n_ff, ff_block = wdn.shape[0], wdn.shape[1]
    const = lambda *shape: pl.BlockSpec(shape, lambda i: (0,) * len(shape),
                                        pipeline_mode=pl.Buffered(1))
    tok = lambda width: pl.BlockSpec(
        (1, tile, width), lambda i: (i // tiles_per_seq, i % tiles_per_seq, 0))
    return pl.pallas_call(
        functools.partial(_ffn_kernel, tiles_per_seq=tiles_per_seq),
        grid=(bsz * tiles_per_seq,),
        in_specs=[tok(d_model), tok(oa.shape[-1]), tok(ob.shape[-1]), const(*gb.shape),
                  const(*wo.shape), const(*gpm.shape), const(*gpf.shape), const(*wup.shape),
                  const(*cw.shape), const(*cb.shape), const(*wdn.shape), const(*gpo.shape)],
        out_specs=tok(d_model),
        out_shape=jax.ShapeDtypeStruct(x.shape, F32),
        scratch_shapes=[pltpu.VMEM((tile, d_model), BF16),
                        pltpu.VMEM((2, tile + SUBLANES, ff_block), F32),
                        pltpu.VMEM((2, n_ff, SUBLANES, ff_block), F32),
                        pltpu.VMEM((tile, d_model), F32)],
        compiler_params=pltpu.CompilerParams(
            dimension_semantics=("arbitrary",), vmem_limit_bytes=VMEM_LIMIT_BYTES),
        name="ffn",
    )(x, oa, ob, gb, wo, gpm, gpf, wup, cw, cb, wdn, gpo)


def _rope_tables(seq):
    half = ROT_DIM // 2
    inv = ROPE_THETA ** (-jnp.arange(0, ROT_DIM, 2, dtype=F32) / ROT_DIM)
    ang = jnp.arange(seq, dtype=F32)[:, None] * inv[None, :]
    cos, sin = jnp.cos(ang), jnp.sin(ang)
    ones = jnp.ones((seq, HEAD_DIM - ROT_DIM), F32)
    zeros = jnp.zeros((seq, HEAD_DIM - ROT_DIM), F32)
    zh = jnp.zeros((seq, half), F32)
    per_head = lambda parts: jnp.tile(jnp.concatenate(parts, axis=1), (1, LANES // HEAD_DIM))
    return (per_head([cos, cos, ones]), per_head([-sin, zh, zeros]), per_head([zh, sin, zeros]))


def kernel(x, pre_mix_norm, w_in, v_norm_g, v_norm_b, w_spatial, b_spatial, out_norm_a, out_norm_b, w_out, post_mix_norm, pre_ffn_norm, w_up, conv_w, conv_b, w_down, post_ffn_norm):
    depth = w_in.shape[0]
    seq = x.shape[1]
    d_ff = w_down.shape[1]
    n_ff = d_ff // FF_BLOCK
    cos_t, sa_t, sb_t = _rope_tables(seq)
    row = lambda p: p[None, :]
    for l in range(depth):
        wup = w_up[l].astype(BF16).reshape(-1, 2, n_ff, FF_BLOCK).transpose(1, 2, 0, 3)
        cw = conv_w[l].reshape(CONV_WIDTH, 2, n_ff, FF_BLOCK).transpose(1, 2, 0, 3)
        cb = conv_b[l].reshape(2, n_ff, 1, FF_BLOCK)
        wdn = w_down[l].astype(BF16).reshape(n_ff, FF_BLOCK, -1)

        oa, q, k, v = _in_proj(x, row(pre_mix_norm[l]), w_in[l].astype(BF16), row(v_norm_g[l]),
                               row(v_norm_b[l]), w_spatial[l], b_spatial[l].T,
                               row(out_norm_a[l]), cos_t, sa_t, sb_t)
        ob = _attn(q, k, v)
        x = _ffn(x, oa, ob, row(out_norm_b[l]), w_out[l].astype(BF16), row(post_mix_norm[l]),
                 row(pre_ffn_norm[l]), wup, cw, cb, wdn, row(post_ffn_norm[l]))
    return x
```

```python
import functools

import numpy as np
import jax
import jax.numpy as jnp
from jax import lax
from jax.experimental import pallas as pl
from jax.experimental.pallas import tpu as pltpu

F32 = jnp.float32
BF16 = jnp.bfloat16

A_GROUPS = 4
CHUNK = 128
HEAD_DIM = 64
ROT_DIM = 16
ROPE_THETA = 500000.0
DILATIONS = (1, 4, 16)
BAND = 128
CONV_WIDTH = 3
EPS = 1e-6
NEG_INF = -1e30

LANES = 128
SUBLANES = 8
VMEM_LIMIT_BYTES = 56 * 1024 * 1024

IN_TILE = 512
FFN_TILE = 512
FF_BLOCK = 512
ATTN_UNROLL = 4


def _rms(x, g):
    return x * lax.rsqrt(jnp.mean(x * x, axis=-1, keepdims=True) + EPS) * g


def _in_proj_kernel(x_ref, g_ref, w_ref, vg_ref, vb_ref, ws_ref, bs_ref, ga_ref,
                    cos_ref, sa_ref, sb_ref, oa_ref, q_ref, k_ref, v_ref):
    a_width = oa_ref.shape[-1]
    b_width = q_ref.shape[-1]
    tile = x_ref.shape[1]

    h = _rms(x_ref[0], g_ref[...]).astype(BF16)
    proj = jnp.dot(h, w_ref[...], preferred_element_type=F32)

    za = proj[:, :2 * a_width]
    za = 0.5 * za * (1.0 + lax.erf(za * np.float32(np.sqrt(0.5))))
    u, va = za[:, :a_width], za[:, a_width:]
    mu = jnp.mean(va, axis=-1, keepdims=True)
    vc = va - mu
    va = vc * lax.rsqrt(jnp.mean(vc * vc, axis=-1, keepdims=True) + EPS) * vg_ref[...] + vb_ref[...]
    va = va.astype(BF16)

    row = lax.broadcasted_iota(jnp.int32, (CHUNK, CHUNK), 0)
    col = lax.broadcasted_iota(jnp.int32, (CHUNK, CHUNK), 1)
    causal = col <= row
    cols = []
    for g in range(A_GROUPS):
        wsg = jnp.where(causal, ws_ref[g], 0.0).astype(BF16)
        bias = bs_ref[:, g:g + 1]
        lo = g * CHUNK
        rows = []
        for n in range(tile // CHUNK):
            vblk = va[n * CHUNK:(n + 1) * CHUNK, lo:lo + CHUNK]
            rows.append(jnp.dot(wsg, vblk, preferred_element_type=F32) + bias)
        cols.append(jnp.concatenate(rows, axis=0))
    o_a = u * jnp.concatenate(cols, axis=1)
    oa_ref[0] = _rms(o_a, ga_ref[...]).astype(oa_ref.dtype)

    cos, sa, sb = cos_ref[...], sa_ref[...], sb_ref[...]
    half = ROT_DIM // 2

    def rope(t):
        out = []
        for c in range(b_width // LANES):
            tc = t[:, c * LANES:(c + 1) * LANES]
            out.append(tc * cos + pltpu.roll(tc, LANES - half, axis=1) * sa
                       + pltpu.roll(tc, half, axis=1) * sb)
        return jnp.concatenate(out, axis=1)

    base = 2 * a_width
    q_ref[0] = rope(proj[:, base:base + b_width]) * np.float32(HEAD_DIM ** -0.5)
    k_ref[0] = rope(proj[:, base + b_width:base + 2 * b_width])
    v_ref[0] = proj[:, base + 2 * b_width:]


def _in_proj(x, g, w, vg, vb, ws, bs_t, ga, cos_t, sa_t, sb_t):
    bsz, seq, d_model = x.shape
    a_width = vg.shape[-1]
    b_width = (w.shape[1] - 2 * a_width) // 3
    tile = IN_TILE
    const = lambda *shape: pl.BlockSpec(shape, lambda b, t: (0,) * len(shape),
                                        pipeline_mode=pl.Buffered(1))
    tok = lambda width: pl.BlockSpec((1, tile, width), lambda b, t: (b, t, 0))
    pos = pl.BlockSpec((tile, LANES), lambda b, t: (t, 0))
    return pl.pallas_call(
        _in_proj_kernel,
        grid=(bsz, seq // tile),
        in_specs=[tok(d_model), const(1, d_model), const(*w.shape), const(1, a_width),
                  const(1, a_width), const(*ws.shape), const(*bs_t.shape), const(1, a_width),
                  pos, pos, pos],
        out_specs=[tok(a_width), tok(b_width), tok(b_width), tok(b_width)],
        out_shape=[jax.ShapeDtypeStruct((bsz, seq, a_width), BF16),
                   jax.ShapeDtypeStruct((bsz, seq, b_width), F32),
                   jax.ShapeDtypeStruct((bsz, seq, b_width), F32),
                   jax.ShapeDtypeStruct((bsz, seq, b_width), F32)],
        compiler_params=pltpu.CompilerParams(
            dimension_semantics=("parallel", "parallel"), vmem_limit_bytes=VMEM_LIMIT_BYTES),
        name="in_proj",
    )(x, g, w, vg, vb, ws, bs_t, ga, cos_t, sa_t, sb_t)


def _attn_kernel(q_ref, k_ref, v_ref, o_ref, acc_ref, m_ref, l_ref, bias_ref):
    seq = q_ref.shape[1]
    lane = lax.broadcasted_iota(jnp.int32, (BAND, LANES), 1)
    head0 = lane < HEAD_DIM
    qi = lax.broadcasted_iota(jnp.int32, (BAND, 2 * BAND), 0)
    kj = lax.broadcasted_iota(jnp.int32, (BAND, 2 * BAND), 1)
    dist = qi + BAND - kj
    bias_ref[...] = jnp.where((dist >= 0) & (dist <= BAND), 0.0, NEG_INF).astype(F32)
    ones = jnp.ones((BAND, LANES), BF16)

    def tile(rows_q, rows_prev):
        qb = q_ref[0, rows_q, :].astype(BF16)
        zero = jnp.zeros_like(qb)
        q2 = jnp.concatenate([jnp.where(head0, qb, zero), jnp.where(head0, zero, qb)], axis=0)
        if rows_prev is None:
            kk = k_ref[0, rows_q, :].astype(BF16)
            vv = jnp.concatenate([v_ref[0, rows_q, :].astype(BF16), ones], axis=1)
            bias = bias_ref[:, BAND:]
        else:
            kk = jnp.concatenate([k_ref[0, rows_prev, :], k_ref[0, rows_q, :]], axis=0).astype(BF16)
            vv = jnp.concatenate([v_ref[0, rows_prev, :], v_ref[0, rows_q, :]], axis=0).astype(BF16)
            vv = jnp.concatenate([vv, jnp.concatenate([ones, ones], axis=0)], axis=1)
            bias = bias_ref[...]
        s = lax.dot_general(q2, kk, (((1,), (1,)), ((), ())), preferred_element_type=F32)
        s = s + jnp.concatenate([bias, bias], axis=0)
        m = jnp.max(s, axis=-1, keepdims=True)
        p = jnp.exp(s - m)
        o2 = jnp.dot(p.astype(BF16), vv, preferred_element_type=F32)
        acc = jnp.where(head0, o2[:BAND, :LANES], o2[BAND:, :LANES])
        lf = jnp.where(head0, o2[:BAND, LANES:], o2[BAND:, LANES:])
        mf = jnp.where(head0, jnp.broadcast_to(m[:BAND], (BAND, LANES)),
                       jnp.broadcast_to(m[BAND:], (BAND, LANES)))
        return acc, mf, lf

    for slot, d in enumerate(DILATIONS[1:]):
        n_blk = seq // (d * BAND)
        groups = d // ATTN_UNROLL

        def dilated(g, blk, slot=slot, d=d):
            for u in range(ATTN_UNROLL):
                start = blk * (BAND * d) + g * ATTN_UNROLL + u
                rows_q = pl.ds(start, BAND, stride=d)
                first = isinstance(blk, int) and blk == 0
                rows_prev = None if first else pl.ds(start - BAND * d, BAND, stride=d)
                acc, mf, lf = tile(rows_q, rows_prev)
                acc_ref[slot, rows_q, :] = acc
                m_ref[slot, rows_q, :] = mf
                l_ref[slot, rows_q, :] = lf

        def first_body(g, carry, dilated=dilated):
            dilated(g, 0)
            return carry

        def rest_body(i, carry, dilated=dilated, groups=groups):
            dilated(i % groups, 1 + i // groups)
            return carry

        lax.fori_loop(0, groups, first_body, 0)
        lax.fori_loop(0, (n_blk - 1) * groups, rest_body, 0)

    def merge(n0, first):
        for u in range(ATTN_UNROLL):
            n = n0 + u
            start = n * BAND if isinstance(n, int) else pl.multiple_of(n * BAND, BAND)
            rows = pl.ds(start, BAND)
            rows_prev = None if (first and u == 0) else pl.ds(start - BAND, BAND)
            acc1, m1, l1 = tile(rows, rows_prev)
            m2, m3 = m_ref[0, rows, :], m_ref[1, rows, :]
            m = jnp.maximum(jnp.maximum(m1, m2), m3)
            w1, w2, w3 = jnp.exp(m1 - m), jnp.exp(m2 - m), jnp.exp(m3 - m)
            num = w1 * acc1 + w2 * acc_ref[0, rows, :] + w3 * acc_ref[1, rows, :]
            den = w1 * l1 + w2 * l_ref[0, rows, :] + w3 * l_ref[1, rows, :]
            o_ref[0, rows, :] = num / den

    def merge_body(i, carry):
        merge(i * ATTN_UNROLL, False)
        return carry

    merge(0, True)
    lax.fori_loop(1, seq // (BAND * ATTN_UNROLL), merge_body, 0)


def _attn(q, k, v):
    bsz, seq, b_width = q.shape
    spec = pl.BlockSpec((1, seq, LANES), lambda b, c: (b, 0, c))
    n_dilated = len(DILATIONS) - 1
    return pl.pallas_call(
        _attn_kernel,
        grid=(bsz, b_width // LANES),
        in_specs=[spec, spec, spec],
        out_specs=spec,
        out_shape=jax.ShapeDtypeStruct((bsz, seq, b_width), F32),
        scratch_shapes=[pltpu.VMEM((n_dilated, seq, LANES), F32)] * 3
        + [pltpu.VMEM((BAND, 2 * BAND), F32)],
        compiler_params=pltpu.CompilerParams(
            dimension_semantics=("parallel", "parallel"), vmem_limit_bytes=VMEM_LIMIT_BYTES),
        name="attn",
    )(q, k, v)


def _ffn_kernel(x_ref, oa_ref, ob_ref, gb_ref, wo_ref, gpm_ref, gpf_ref, wup_ref, cw_ref,
                cb_ref, wdn_ref, gpo_ref, out_ref, h_ref, up_ref, carry_ref, acc_ref,
                *, tiles_per_seq):
    tile = x_ref.shape[1]
    n_ff, ff_block = wdn_ref.shape[0], wdn_ref.shape[1]
    first_tile = pl.program_id(0) % tiles_per_seq == 0

    mixed = jnp.concatenate([oa_ref[0], _rms(ob_ref[0], gb_ref[...]).astype(BF16)], axis=1)
    y = jnp.dot(mixed, wo_ref[...], preferred_element_type=F32)
    x1 = x_ref[0] + _rms(y, gpm_ref[...])
    out_ref[0] = x1
    h_ref[...] = _rms(x1, gpf_ref[...]).astype(BF16)
    acc_ref[...] = jnp.zeros_like(acc_ref)

    def conv(part, j, up):
        prev = carry_ref[part, j]
        up_ref[part, pl.ds(0, SUBLANES), :] = jnp.where(first_tile, jnp.zeros_like(prev), prev)
        up_ref[part, pl.ds(SUBLANES, tile), :] = up
        carry_ref[part, j] = up[tile - SUBLANES:]
        w = cw_ref[part, j]
        out = cb_ref[part, j]
        for i in range(CONV_WIDTH):
            shift = SUBLANES - (CONV_WIDTH - 1) + i
            out = out + w[i:i + 1] * up_ref[part, pl.ds(shift, tile), :]
        return out

    def body(j, carry):
        h = h_ref[...]
        gate = conv(0, j, jnp.dot(h, wup_ref[0, j], preferred_element_type=F32))
        val = conv(1, j, jnp.dot(h, wup_ref[1, j], preferred_element_type=F32))
        cdf = 0.5 * (1.0 + jnp.tanh(np.float32(np.sqrt(2.0 / np.pi))
                                    * (gate + np.float32(0.044715) * (gate * gate * gate))))
        act = (gate * cdf * val).astype(BF16)
        acc_ref[...] += jnp.dot(act, wdn_ref[j], preferred_element_type=F32)
        return carry

    lax.fori_loop(0, n_ff, body, 0)
    out_ref[0] = out_ref[0] + _rms(acc_ref[...], gpo_ref[...])


def _ffn(x, oa, ob, gb, wo, gpm, gpf, wup, cw, cb, wdn, gpo):
    bsz, seq, d_model = x.shape
    tile = FFN_TILE
    tiles_per_seq = seq // tile
    n_ff, ff_block = wdn.shape[0], wdn.shape[1]
    const = lambda *shape: pl.BlockSpec(shape, lambda i: (0,) * len(shape),
                                        pipeline_mode=pl.Buffered(1))
    tok = lambda width: pl.BlockSpec(
        (1, tile, width), lambda i: (i // tiles_per_seq, i % tiles_per_seq, 0))
    return pl.pallas_call(
        functools.partial(_ffn_kernel, tiles_per_seq=tiles_per_seq),
        grid=(bsz * tiles_per_seq,),
        in_specs=[tok(d_model), tok(oa.shape[-1]), tok(ob.shape[-1]), const(*gb.shape),
                  const(*wo.shape), const(*gpm.shape), const(*gpf.shape), const(*wup.shape),
                  const(*cw.shape), const(*cb.shape), const(*wdn.shape), const(*gpo.shape)],
        out_specs=tok(d_model),
        out_shape=jax.ShapeDtypeStruct(x.shape, F32),
        scratch_shapes=[pltpu.VMEM((tile, d_model), BF16),
                        pltpu.VMEM((2, tile + SUBLANES, ff_block), F32),
                        pltpu.VMEM((2, n_ff, SUBLANES, ff_block), F32),
                        pltpu.VMEM((tile, d_model), F32)],
        compiler_params=pltpu.CompilerParams(
            dimension_semantics=("arbitrary",), vmem_limit_bytes=VMEM_LIMIT_BYTES),
        name="ffn",
    )(x, oa, ob, gb, wo, gpm, gpf, wup, cw, cb, wdn, gpo)


def _rope_tables(seq):
    half = ROT_DIM // 2
    inv = ROPE_THETA ** (-jnp.arange(0, ROT_DIM, 2, dtype=F32) / ROT_DIM)
    ang = jnp.arange(seq, dtype=F32)[:, None] * inv[None, :]
    cos, sin = jnp.cos(ang), jnp.sin(ang)
    ones = jnp.ones((seq, HEAD_DIM - ROT_DIM), F32)
    zeros = jnp.zeros((seq, HEAD_DIM - ROT_DIM), F32)
    zh = jnp.zeros((seq, half), F32)
    per_head = lambda parts: jnp.tile(jnp.concatenate(parts, axis=1), (1, LANES // HEAD_DIM))
    return (per_head([cos, cos, ones]), per_head([-sin, zh, zeros]), per_head([zh, sin, zeros]))


def kernel(x, pre_mix_norm, w_in, v_norm_g, v_norm_b, w_spatial, b_spatial, out_norm_a, out_norm_b, w_out, post_mix_norm, pre_ffn_norm, w_up, conv_w, conv_b, w_down, post_ffn_norm):
    depth = w_in.shape[0]
    seq = x.shape[1]
    d_ff = w_down.shape[1]
    n_ff = d_ff // FF_BLOCK
    cos_t, sa_t, sb_t = _rope_tables(seq)
    row = lambda p: p[None, :]
    for l in range(depth):
        wup = w_up[l].astype(BF16).reshape(-1, 2, n_ff, FF_BLOCK).transpose(1, 2, 0, 3)
        cw = conv_w[l].reshape(CONV_WIDTH, 2, n_ff, FF_BLOCK).transpose(1, 2, 0, 3)
        cb = conv_b[l].reshape(2, n_ff, 1, FF_BLOCK)
        wdn = w_down[l].astype(BF16).reshape(n_ff, FF_BLOCK, -1)

        oa, q, k, v = _in_proj(x, row(pre_mix_norm[l]), w_in[l].astype(BF16), row(v_norm_g[l]),
                               row(v_norm_b[l]), w_spatial[l], b_spatial[l].T,
                               row(out_norm_a[l]), cos_t, sa_t, sb_t)
        ob = _attn(q, k, v)
        x = _ffn(x, oa, ob, row(out_norm_b[l]), w_out[l].astype(BF16), row(post_mix_norm[l]),
                 row(pre_ffn_norm[l]), wup, cw, cb, wdn, row(post_ffn_norm[l]))
    return x
```

```python
import functools

import numpy as np
import jax
import jax.numpy as jnp
from jax import lax
from jax.experimental import pallas as pl
from jax.experimental.pallas import tpu as pltpu

F32 = jnp.float32
BF16 = jnp.bfloat16

A_GROUPS = 4
CHUNK = 128
HEAD_DIM = 64
ROT_DIM = 16
ROPE_THETA = 500000.0
DILATIONS = (1, 4, 16)
BAND = 128
CONV_WIDTH = 3
EPS = 1e-6
NEG_INF = -1e30

LANES = 128
SUBLANES = 8
VMEM_LIMIT_BYTES = 56 * 1024 * 1024

IN_TILE = 512
FFN_TILE = 512
FF_BLOCK = 256
FF_DOWN_BLOCK = 512
FF_GROUP = 8
ATTN_UNROLL = 4


def _rms(x, g):
    return x * lax.rsqrt(jnp.mean(x * x, axis=-1, keepdims=True) + EPS) * g


def _in_proj_kernel(x_ref, g_ref, w_ref, vg_ref, vb_ref, ws_ref, bs_ref, ga_ref,
                    cos_ref, sa_ref, sb_ref, oa_ref, q_ref, k_ref, v_ref):
    a_width = oa_ref.shape[-1]
    b_width = q_ref.shape[-1]
    tile = x_ref.shape[1]

    h = _rms(x_ref[0], g_ref[...]).astype(BF16)
    proj = jnp.dot(h, w_ref[...], preferred_element_type=F32)

    za = proj[:, :2 * a_width]
    za = 0.5 * za * (1.0 + lax.erf(za * np.float32(np.sqrt(0.5))))
    u, va = za[:, :a_width], za[:, a_width:]
    mu = jnp.mean(va, axis=-1, keepdims=True)
    vc = va - mu
    va = vc * lax.rsqrt(jnp.mean(vc * vc, axis=-1, keepdims=True) + EPS) * vg_ref[...] + vb_ref[...]
    va = va.astype(BF16)

    row = lax.broadcasted_iota(jnp.int32, (CHUNK, CHUNK), 0)
    col = lax.broadcasted_iota(jnp.int32, (CHUNK, CHUNK), 1)
    causal = col <= row
    cols = []
    for g in range(A_GROUPS):
        wsg = jnp.where(causal, ws_ref[g], 0.0).astype(BF16)
        bias = bs_ref[:, g:g + 1]
        lo = g * CHUNK
        rows = []
        for n in range(tile // CHUNK):
            vblk = va[n * CHUNK:(n + 1) * CHUNK, lo:lo + CHUNK]
            rows.append(jnp.dot(wsg, vblk, preferred_element_type=F32) + bias)
        cols.append(jnp.concatenate(rows, axis=0))
    o_a = u * jnp.concatenate(cols, axis=1)
    oa_ref[0] = _rms(o_a, ga_ref[...]).astype(oa_ref.dtype)

    cos, sa, sb = cos_ref[...], sa_ref[...], sb_ref[...]
    half = ROT_DIM // 2

    def rope(t):
        out = []
        for c in range(b_width // LANES):
            tc = t[:, c * LANES:(c + 1) * LANES]
            out.append(tc * cos + pltpu.roll(tc, LANES - half, axis=1) * sa
                       + pltpu.roll(tc, half, axis=1) * sb)
        return jnp.concatenate(out, axis=1)

    base = 2 * a_width
    q_ref[0] = rope(proj[:, base:base + b_width]) * np.float32(HEAD_DIM ** -0.5)
    k_ref[0] = rope(proj[:, base + b_width:base + 2 * b_width])
    v_ref[0] = proj[:, base + 2 * b_width:]


def _in_proj(x, g, w, vg, vb, ws, bs_t, ga, cos_t, sa_t, sb_t):
    bsz, seq, d_model = x.shape
    a_width = vg.shape[-1]
    b_width = (w.shape[1] - 2 * a_width) // 3
    tile = IN_TILE
    const = lambda *shape: pl.BlockSpec(shape, lambda b, t: (0,) * len(shape),
                                        pipeline_mode=pl.Buffered(1))
    tok = lambda width: pl.BlockSpec((1, tile, width), lambda b, t: (b, t, 0))
    pos = pl.BlockSpec((tile, LANES), lambda b, t: (t, 0))
    return pl.pallas_call(
        _in_proj_kernel,
        grid=(bsz, seq // tile),
        in_specs=[tok(d_model), const(1, d_model), const(*w.shape), const(1, a_width),
                  const(1, a_width), const(*ws.shape), const(*bs_t.shape), const(1, a_width),
                  pos, pos, pos],
        out_specs=[tok(a_width), tok(b_width), tok(b_width), tok(b_width)],
        out_shape=[jax.ShapeDtypeStruct((bsz, seq, a_width), BF16),
                   jax.ShapeDtypeStruct((bsz, seq, b_width), F32),
                   jax.ShapeDtypeStruct((bsz, seq, b_width), F32),
                   jax.ShapeDtypeStruct((bsz, seq, b_width), F32)],
        compiler_params=pltpu.CompilerParams(
            dimension_semantics=("parallel", "parallel"), vmem_limit_bytes=VMEM_LIMIT_BYTES),
        name="in_proj",
    )(x, g, w, vg, vb, ws, bs_t, ga, cos_t, sa_t, sb_t)


def _attn_kernel(q_ref, k_ref, v_ref, o_ref, acc_ref, m_ref, l_ref, bias_ref):
    seq = q_ref.shape[1]
    lane = lax.broadcasted_iota(jnp.int32, (BAND, LANES), 1)
    head0 = lane < HEAD_DIM
    qi = lax.broadcasted_iota(jnp.int32, (BAND, 2 * BAND), 0)
    kj = lax.broadcasted_iota(jnp.int32, (BAND, 2 * BAND), 1)
    dist = qi + BAND - kj
    bias_ref[...] = jnp.where((dist >= 0) & (dist <= BAND), 0.0, NEG_INF).astype(F32)
    ones = jnp.ones((BAND, LANES), BF16)

    def tile(rows_q, rows_prev):
        qb = q_ref[0, rows_q, :].astype(BF16)
        zero = jnp.zeros_like(qb)
        q2 = jnp.concatenate([jnp.where(head0, qb, zero), jnp.where(head0, zero, qb)], axis=0)
        if rows_prev is None:
            kk = k_ref[0, rows_q, :].astype(BF16)
            vv = jnp.concatenate([v_ref[0, rows_q, :].astype(BF16), ones], axis=1)
            bias = bias_ref[:, BAND:]
        else:
            kk = jnp.concatenate([k_ref[0, rows_prev, :], k_ref[0, rows_q, :]], axis=0).astype(BF16)
            vv = jnp.concatenate([v_ref[0, rows_prev, :], v_ref[0, rows_q, :]], axis=0).astype(BF16)
            vv = jnp.concatenate([vv, jnp.concatenate([ones, ones], axis=0)], axis=1)
            bias = bias_ref[...]
        s = lax.dot_general(q2, kk, (((1,), (1,)), ((), ())), preferred_element_type=F32)
        s = s + jnp.concatenate([bias, bias], axis=0)
        m = jnp.max(s, axis=-1, keepdims=True)
        p = jnp.exp(s - m)
        o2 = jnp.dot(p.astype(BF16), vv, preferred_element_type=F32)
        acc = jnp.where(head0, o2[:BAND, :LANES], o2[BAND:, :LANES])
        lf = jnp.where(head0, o2[:BAND, LANES:], o2[BAND:, LANES:])
        mf = jnp.where(head0, jnp.broadcast_to(m[:BAND], (BAND, LANES)),
                       jnp.broadcast_to(m[BAND:], (BAND, LANES)))
        return acc, mf, lf

    for slot, d in enumerate(DILATIONS[1:]):
        n_blk = seq // (d * BAND)
        groups = d // ATTN_UNROLL

        def dilated(g, blk, slot=slot, d=d):
            for u in range(ATTN_UNROLL):
                start = blk * (BAND * d) + g * ATTN_UNROLL + u
                rows_q = pl.ds(start, BAND, stride=d)
                first = isinstance(blk, int) and blk == 0
                rows_prev = None if first else pl.ds(start - BAND * d, BAND, stride=d)
                acc, mf, lf = tile(rows_q, rows_prev)
                acc_ref[slot, rows_q, :] = acc
                m_ref[slot, rows_q, :] = mf
                l_ref[slot, rows_q, :] = lf

        def first_body(g, carry, dilated=dilated):
            dilated(g, 0)
            return carry

        def rest_body(i, carry, dilated=dilated, groups=groups):
            dilated(i % groups, 1 + i // groups)
            return carry

        lax.fori_loop(0, groups, first_body, 0)
        lax.fori_loop(0, (n_blk - 1) * groups, rest_body, 0)

    def merge(n0, first):
        for u in range(ATTN_UNROLL):
            n = n0 + u
            start = n * BAND if isinstance(n, int) else pl.multiple_of(n * BAND, BAND)
            rows = pl.ds(start, BAND)
            rows_prev = None if (first and u == 0) else pl.ds(start - BAND, BAND)
            acc1, m1, l1 = tile(rows, rows_prev)
            m2, m3 = m_ref[0, rows, :], m_ref[1, rows, :]
            m = jnp.maximum(jnp.maximum(m1, m2), m3)
            w1, w2, w3 = jnp.exp(m1 - m), jnp.exp(m2 - m), jnp.exp(m3 - m)
            num = w1 * acc1 + w2 * acc_ref[0, rows, :] + w3 * acc_ref[1, rows, :]
            den = w1 * l1 + w2 * l_ref[0, rows, :] + w3 * l_ref[1, rows, :]
            o_ref[0, rows, :] = num / den

    def merge_body(i, carry):
        merge(i * ATTN_UNROLL, False)
        return carry

    merge(0, True)
    lax.fori_loop(1, seq // (BAND * ATTN_UNROLL), merge_body, 0)


def _attn(q, k, v):
    bsz, seq, b_width = q.shape
    spec = pl.BlockSpec((1, seq, LANES), lambda b, c: (b, 0, c))
    n_dilated = len(DILATIONS) - 1
    return pl.pallas_call(
        _attn_kernel,
        grid=(bsz, b_width // LANES),
        in_specs=[spec, spec, spec],
        out_specs=spec,
        out_shape=jax.ShapeDtypeStruct((bsz, seq, b_width), F32),
        scratch_shapes=[pltpu.VMEM((n_dilated, seq, LANES), F32)] * 3
        + [pltpu.VMEM((BAND, 2 * BAND), F32)],
        compiler_params=pltpu.CompilerParams(
            dimension_semantics=("parallel", "parallel"), vmem_limit_bytes=VMEM_LIMIT_BYTES),
        name="attn",
    )(q, k, v)


def _ffn_kernel(x_ref, oa_ref, ob_ref, gb_ref, wo_ref, gpm_ref, gpf_ref, wup_ref, cw_ref,
                cb_ref, wdn_ref, gpo_ref, out_ref, h_ref, up_ref, act_ref, carry_ref, acc_ref,
                *, tiles_per_seq):
    tile = x_ref.shape[1]
    n_up, ff_block = wup_ref.shape[1], wup_ref.shape[3]
    per_down = wdn_ref.shape[1] // ff_block
    first_tile = pl.program_id(0) % tiles_per_seq == 0

    mixed = jnp.concatenate([oa_ref[0], _rms(ob_ref[0], gb_ref[...]).astype(BF16)], axis=1)
    y = jnp.dot(mixed, wo_ref[...], preferred_element_type=F32)
    x1 = x_ref[0] + _rms(y, gpm_ref[...])
    out_ref[0] = x1
    h_ref[...] = _rms(x1, gpf_ref[...]).astype(BF16)
    acc_ref[...] = jnp.zeros_like(acc_ref)

    def up_proj(j, sub):
        h = h_ref[...]
        for part in range(2):
            up = jnp.dot(h, wup_ref[part, j], preferred_element_type=F32)
            buf = up_ref.at[sub, part]
            prev = carry_ref[part, j]
            buf[pl.ds(0, SUBLANES), :] = jnp.where(first_tile, jnp.zeros_like(prev), prev)
            buf[pl.ds(SUBLANES, tile), :] = up
            carry_ref[part, j] = up[tile - SUBLANES:]

    def activate(j, sub):
        conv = []
        for part in range(2):
            buf = up_ref.at[sub, part]
            w = cw_ref[part, j]
            out = cb_ref[part, j]
            for i in range(CONV_WIDTH):
                shift = SUBLANES - (CONV_WIDTH - 1) + i
                out = out + w[i:i + 1] * buf[pl.ds(shift, tile), :]
            conv.append(out)
        gate, val = conv
        cdf = 0.5 * (1.0 + jnp.tanh(np.float32(np.sqrt(2.0 / np.pi))
                                    * (gate + np.float32(0.044715) * (gate * gate * gate))))
        lo = (sub % per_down) * ff_block
        act_ref[sub // per_down, :, lo:lo + ff_block] = (gate * cdf * val).astype(BF16)

    def body(g, carry):
        for sub in range(FF_GROUP):
            up_proj(g * FF_GROUP + sub, sub)
        for sub in range(FF_GROUP):
            activate(g * FF_GROUP + sub, sub)
        for p in range(FF_GROUP // per_down):
            acc_ref[...] += jnp.dot(act_ref[p], wdn_ref[g * (FF_GROUP // per_down) + p],
                                    preferred_element_type=F32)
        return carry

    lax.fori_loop(0, n_up // FF_GROUP, body, 0)
    out_ref[0] = out_ref[0] + _rms(acc_ref[...], gpo_ref[...])


def _ffn(x, oa, ob, gb, wo, gpm, gpf, wup, cw, cb, wdn, gpo):
    bsz, seq, d_model = x.shape
    tile = FFN_TILE
    tiles_per_seq = seq // tile
    n_up, ff_block = wup.shape[1], wup.shape[3]
    const = lambda *shape: pl.BlockSpec(shape, lambda i: (0,) * len(shape),
                                        pipeline_mode=pl.Buffered(1))
    tok = lambda width: pl.BlockSpec(
        (1, tile, width), lambda i: (i // tiles_per_seq, i % tiles_per_seq, 0))
    return pl.pallas_call(
        functools.partial(_ffn_kernel, tiles_per_seq=tiles_per_seq),
        grid=(bsz * tiles_per_seq,),
        in_specs=[tok(d_model), tok(oa.shape[-1]), tok(ob.shape[-1]), const(*gb.shape),
                  const(*wo.shape), const(*gpm.shape), const(*gpf.shape), const(*wup.shape),
                  const(*cw.shape), const(*cb.shape), const(*wdn.shape), const(*gpo.shape)],
        out_specs=tok(d_model),
        out_shape=jax.ShapeDtypeStruct(x.shape, F32),
        scratch_shapes=[pltpu.VMEM((tile, d_model), BF16),
                        pltpu.VMEM((FF_GROUP, 2, tile + SUBLANES, ff_block), F32),
                        pltpu.VMEM((FF_GROUP * ff_block // wdn.shape[1], tile, wdn.shape[1]), BF16),
                        pltpu.VMEM((2, n_up, SUBLANES, ff_block), F32),
                        pltpu.VMEM((tile, d_model), F32)],
        compiler_params=pltpu.CompilerParams(
            dimension_semantics=("arbitrary",), vmem_limit_bytes=VMEM_LIMIT_BYTES),
        name="ffn",
    )(x, oa, ob, gb, wo, gpm, gpf, wup, cw, cb, wdn, gpo)


def _rope_tables(seq):
    half = ROT_DIM // 2
    inv = ROPE_THETA ** (-jnp.arange(0, ROT_DIM, 2, dtype=F32) / ROT_DIM)
    ang = jnp.arange(seq, dtype=F32)[:, None] * inv[None, :]
    cos, sin = jnp.cos(ang), jnp.sin(ang)
    ones = jnp.ones((seq, HEAD_DIM - ROT_DIM), F32)
    zeros = jnp.zeros((seq, HEAD_DIM - ROT_DIM), F32)
    zh = jnp.zeros((seq, half), F32)
    per_head = lambda parts: jnp.tile(jnp.concatenate(parts, axis=1), (1, LANES // HEAD_DIM))
    return (per_head([cos, cos, ones]), per_head([-sin, zh, zeros]), per_head([zh, sin, zeros]))


def kernel(x, pre_mix_norm, w_in, v_norm_g, v_norm_b, w_spatial, b_spatial, out_norm_a, out_norm_b, w_out, post_mix_norm, pre_ffn_norm, w_up, conv_w, conv_b, w_down, post_ffn_norm):
    depth = w_in.shape[0]
    seq = x.shape[1]
    d_ff = w_down.shape[1]
    n_ff = d_ff // FF_BLOCK
    cos_t, sa_t, sb_t = _rope_tables(seq)
    row = lambda p: p[None, :]
    for l in range(depth):
        wup = w_up[l].astype(BF16).reshape(-1, 2, n_ff, FF_BLOCK).transpose(1, 2, 0, 3)
        cw = conv_w[l].reshape(CONV_WIDTH, 2, n_ff, FF_BLOCK).transpose(1, 2, 0, 3)
        cb = conv_b[l].reshape(2, n_ff, 1, FF_BLOCK)
        wdn = w_down[l].astype(BF16).reshape(d_ff // FF_DOWN_BLOCK, FF_DOWN_BLOCK, -1)

        oa, q, k, v = _in_proj(x, row(pre_mix_norm[l]), w_in[l].astype(BF16), row(v_norm_g[l]),
                               row(v_norm_b[l]), w_spatial[l], b_spatial[l].T,
                               row(out_norm_a[l]), cos_t, sa_t, sb_t)
        ob = _attn(q, k, v)
        x = _ffn(x, oa, ob, row(out_norm_b[l]), w_out[l].astype(BF16), row(post_mix_norm[l]),
                 row(pre_ffn_norm[l]), wup, cw, cb, wdn, row(post_ffn_norm[l]))
    return x
```

```python
import functools

import numpy as np
import jax
import jax.numpy as jnp
from jax import lax
from jax.experimental import pallas as pl
from jax.experimental.pallas import tpu as pltpu

F32 = jnp.float32
BF16 = jnp.bfloat16

A_GROUPS = 4
CHUNK = 128
HEAD_DIM = 64
ROT_DIM = 16
ROPE_THETA = 500000.0
DILATIONS = (1, 4, 16)
BAND = 128
CONV_WIDTH = 3
EPS = 1e-6
NEG_INF = -1e30

LANES = 128
SUBLANES = 8
VMEM_LIMIT_BYTES = 56 * 1024 * 1024

IN_TILE = 512
FFN_TILE = 512
FF_BLOCK = 256
FF_DOWN_BLOCK = 512
FF_GROUP = 8
ATTN_UNROLL = 8


def _rms(x, g):
    return x * lax.rsqrt(jnp.mean(x * x, axis=-1, keepdims=True) + EPS) * g


def _in_proj_kernel(x_ref, g_ref, w_ref, vg_ref, vb_ref, ws_ref, bs_ref, ga_ref,
                    cos_ref, sa_ref, sb_ref, oa_ref, q_ref, k_ref, v_ref):
    a_width = oa_ref.shape[-1]
    b_width = q_ref.shape[-1]
    tile = x_ref.shape[1]

    h = _rms(x_ref[0], g_ref[...]).astype(BF16)
    proj = jnp.dot(h, w_ref[...], preferred_element_type=F32)

    za = proj[:, :2 * a_width]
    za = 0.5 * za * (1.0 + lax.erf(za * np.float32(np.sqrt(0.5))))
    u, va = za[:, :a_width], za[:, a_width:]
    mu = jnp.mean(va, axis=-1, keepdims=True)
    vc = va - mu
    va = vc * lax.rsqrt(jnp.mean(vc * vc, axis=-1, keepdims=True) + EPS) * vg_ref[...] + vb_ref[...]
    va = va.astype(BF16)

    row = lax.broadcasted_iota(jnp.int32, (CHUNK, CHUNK), 0)
    col = lax.broadcasted_iota(jnp.int32, (CHUNK, CHUNK), 1)
    causal = col <= row
    cols = []
    for g in range(A_GROUPS):
        wsg = jnp.where(causal, ws_ref[g], 0.0).astype(BF16)
        bias = bs_ref[:, g:g + 1]
        lo = g * CHUNK
        rows = []
        for n in range(tile // CHUNK):
            vblk = va[n * CHUNK:(n + 1) * CHUNK, lo:lo + CHUNK]
            rows.append(jnp.dot(wsg, vblk, preferred_element_type=F32) + bias)
        cols.append(jnp.concatenate(rows, axis=0))
    o_a = u * jnp.concatenate(cols, axis=1)
    oa_ref[0] = _rms(o_a, ga_ref[...]).astype(oa_ref.dtype)

    cos, sa, sb = cos_ref[...], sa_ref[...], sb_ref[...]
    half = ROT_DIM // 2

    def rope(t):
        out = []
        for c in range(b_width // LANES):
            tc = t[:, c * LANES:(c + 1) * LANES]
            out.append(tc * cos + pltpu.roll(tc, LANES - half, axis=1) * sa
                       + pltpu.roll(tc, half, axis=1) * sb)
        return jnp.concatenate(out, axis=1)

    base = 2 * a_width
    q_ref[0] = rope(proj[:, base:base + b_width]) * np.float32(HEAD_DIM ** -0.5)
    k_ref[0] = rope(proj[:, base + b_width:base + 2 * b_width])
    v_ref[0] = proj[:, base + 2 * b_width:]


def _in_proj(x, g, w, vg, vb, ws, bs_t, ga, cos_t, sa_t, sb_t):
    bsz, seq, d_model = x.shape
    a_width = vg.shape[-1]
    b_width = (w.shape[1] - 2 * a_width) // 3
    tile = IN_TILE
    const = lambda *shape: pl.BlockSpec(shape, lambda b, t: (0,) * len(shape),
                                        pipeline_mode=pl.Buffered(1))
    tok = lambda width: pl.BlockSpec((1, tile, width), lambda b, t: (b, t, 0))
    pos = pl.BlockSpec((tile, LANES), lambda b, t: (t, 0))
    return pl.pallas_call(
        _in_proj_kernel,
        grid=(bsz, seq // tile),
        in_specs=[tok(d_model), const(1, d_model), const(*w.shape), const(1, a_width),
                  const(1, a_width), const(*ws.shape), const(*bs_t.shape), const(1, a_width),
                  pos, pos, pos],
        out_specs=[tok(a_width), tok(b_width), tok(b_width), tok(b_width)],
        out_shape=[jax.ShapeDtypeStruct((bsz, seq, a_width), BF16),
                   jax.ShapeDtypeStruct((bsz, seq, b_width), F32),
                   jax.ShapeDtypeStruct((bsz, seq, b_width), F32),
                   jax.ShapeDtypeStruct((bsz, seq, b_width), F32)],
        compiler_params=pltpu.CompilerParams(
            dimension_semantics=("parallel", "parallel"), vmem_limit_bytes=VMEM_LIMIT_BYTES),
        name="in_proj",
    )(x, g, w, vg, vb, ws, bs_t, ga, cos_t, sa_t, sb_t)


def _attn_kernel(q_ref, k_ref, v_ref, o_ref, acc_ref, m_ref, l_ref, bias_ref):
    seq = q_ref.shape[1]
    lane = lax.broadcasted_iota(jnp.int32, (BAND, LANES), 1)
    head0 = lane < HEAD_DIM
    qi = lax.broadcasted_iota(jnp.int32, (BAND, 2 * BAND), 0)
    kj = lax.broadcasted_iota(jnp.int32, (BAND, 2 * BAND), 1)
    dist = qi + BAND - kj
    bias_ref[...] = jnp.where((dist >= 0) & (dist <= BAND), 0.0, NEG_INF).astype(F32)
    ones = jnp.ones((BAND, LANES), BF16)

    def tile(rows_q, rows_prev):
        qb = q_ref[0, rows_q, :].astype(BF16)
        zero = jnp.zeros_like(qb)
        q2 = jnp.concatenate([jnp.where(head0, qb, zero), jnp.where(head0, zero, qb)], axis=0)
        if rows_prev is None:
            kk = k_ref[0, rows_q, :].astype(BF16)
            vv = jnp.concatenate([v_ref[0, rows_q, :].astype(BF16), ones], axis=1)
            bias = bias_ref[:, BAND:]
        else:
            kk = jnp.concatenate([k_ref[0, rows_prev, :], k_ref[0, rows_q, :]], axis=0).astype(BF16)
            vv = jnp.concatenate([v_ref[0, rows_prev, :], v_ref[0, rows_q, :]], axis=0).astype(BF16)
            vv = jnp.concatenate([vv, jnp.concatenate([ones, ones], axis=0)], axis=1)
            bias = bias_ref[...]
        s = lax.dot_general(q2, kk, (((1,), (1,)), ((), ())), preferred_element_type=F32)
        s = s + jnp.concatenate([bias, bias], axis=0)
        m = jnp.max(s, axis=-1, keepdims=True)
        p = jnp.exp(s - m)
        o2 = jnp.dot(p.astype(BF16), vv, preferred_element_type=F32)
        acc = jnp.where(head0, o2[:BAND, :LANES], o2[BAND:, :LANES])
        lf = jnp.where(head0, o2[:BAND, LANES:], o2[BAND:, LANES:])
        mf = jnp.where(head0, jnp.broadcast_to(m[:BAND], (BAND, LANES)),
                       jnp.broadcast_to(m[BAND:], (BAND, LANES)))
        return acc, mf, lf

    def run_tiles(n_tiles, tile_fn, start=0):
        unroll = max(u for u in range(1, ATTN_UNROLL + 1) if (n_tiles - start) % u == 0)
        if n_tiles - start == unroll:
            for u in range(unroll):
                tile_fn(start + u)
            return

        def body(i, carry):
            for u in range(unroll):
                tile_fn(start + i * unroll + u)
            return carry

        lax.fori_loop(0, (n_tiles - start) // unroll, body, 0)

    for slot, d in enumerate(DILATIONS[1:]):
        n_blk = seq // (d * BAND)

        def dilated(start, first, slot=slot, d=d):
            rows_q = pl.ds(start, BAND, stride=d)
            rows_prev = None if first else pl.ds(start - BAND * d, BAND, stride=d)
            acc, mf, lf = tile(rows_q, rows_prev)
            acc_ref[slot, rows_q, :] = acc
            m_ref[slot, rows_q, :] = mf
            l_ref[slot, rows_q, :] = lf

        run_tiles(d, lambda t, dilated=dilated: dilated(t, True))
        run_tiles(d * (n_blk - 1),
                  lambda t, dilated=dilated, d=d: dilated((1 + t // d) * (BAND * d) + t % d, False))

    def merge(n):
        static = isinstance(n, int)
        start = n * BAND if static else pl.multiple_of(n * BAND, BAND)
        rows = pl.ds(start, BAND)
        rows_prev = None if (static and n == 0) else pl.ds(start - BAND, BAND)
        acc1, m1, l1 = tile(rows, rows_prev)
        m2, m3 = m_ref[0, rows, :], m_ref[1, rows, :]
        m = jnp.maximum(jnp.maximum(m1, m2), m3)
        w1, w2, w3 = jnp.exp(m1 - m), jnp.exp(m2 - m), jnp.exp(m3 - m)
        num = w1 * acc1 + w2 * acc_ref[0, rows, :] + w3 * acc_ref[1, rows, :]
        den = w1 * l1 + w2 * l_ref[0, rows, :] + w3 * l_ref[1, rows, :]
        o_ref[0, rows, :] = num / den

    run_tiles(ATTN_UNROLL, merge)
    run_tiles(seq // BAND, merge, start=ATTN_UNROLL)


def _attn(q, k, v):
    bsz, seq, b_width = q.shape
    spec = pl.BlockSpec((1, seq, LANES), lambda b, c: (b, 0, c))
    n_dilated = len(DILATIONS) - 1
    return pl.pallas_call(
        _attn_kernel,
        grid=(bsz, b_width // LANES),
        in_specs=[spec, spec, spec],
        out_specs=spec,
        out_shape=jax.ShapeDtypeStruct((bsz, seq, b_width), F32),
        scratch_shapes=[pltpu.VMEM((n_dilated, seq, LANES), F32)] * 3
        + [pltpu.VMEM((BAND, 2 * BAND), F32)],
        compiler_params=pltpu.CompilerParams(
            dimension_semantics=("parallel", "parallel"), vmem_limit_bytes=VMEM_LIMIT_BYTES),
        name="attn",
    )(q, k, v)


def _ffn_kernel(x_ref, oa_ref, ob_ref, gb_ref, wo_ref, gpm_ref, gpf_ref, wup_ref, cw_ref,
                cb_ref, wdn_ref, gpo_ref, out_ref, h_ref, up_ref, act_ref, carry_ref, acc_ref,
                *, tiles_per_seq):
    tile = x_ref.shape[1]
    n_up, ff_block = wup_ref.shape[1], wup_ref.shape[3]
    per_down = wdn_ref.shape[1] // ff_block
    first_tile = pl.program_id(0) % tiles_per_seq == 0

    mixed = jnp.concatenate([oa_ref[0], _rms(ob_ref[0], gb_ref[...]).astype(BF16)], axis=1)
    y = jnp.dot(mixed, wo_ref[...], preferred_element_type=F32)
    x1 = x_ref[0] + _rms(y, gpm_ref[...])
    out_ref[0] = x1
    h_ref[...] = _rms(x1, gpf_ref[...]).astype(BF16)
    acc_ref[...] = jnp.zeros_like(acc_ref)

    def up_proj(j, sub):
        h = h_ref[...]
        for part in range(2):
            up = jnp.dot(h, wup_ref[part, j], preferred_element_type=F32)
            buf = up_ref.at[sub, part]
            prev = carry_ref[part, j]
            buf[pl.ds(0, SUBLANES), :] = jnp.where(first_tile, jnp.zeros_like(prev), prev)
            buf[pl.ds(SUBLANES, tile), :] = up
            carry_ref[part, j] = up[tile - SUBLANES:]

    def activate(j, sub):
        conv = []
        for part in range(2):
            buf = up_ref.at[sub, part]
            w = cw_ref[part, j]
            out = cb_ref[part, j]
            for i in range(CONV_WIDTH):
                shift = SUBLANES - (CONV_WIDTH - 1) + i
                out = out + w[i:i + 1] * buf[pl.ds(shift, tile), :]
            conv.append(out)
        gate, val = conv
        cdf = 0.5 * (1.0 + jnp.tanh(np.float32(np.sqrt(2.0 / np.pi))
                                    * (gate + np.float32(0.044715) * (gate * gate * gate))))
        lo = (sub % per_down) * ff_block
        act_ref[sub // per_down, :, lo:lo + ff_block] = (gate * cdf * val).astype(BF16)

    def body(g, carry):
        for sub in range(FF_GROUP):
            up_proj(g * FF_GROUP + sub, sub)
        for sub in range(FF_GROUP):
            activate(g * FF_GROUP + sub, sub)
        for p in range(FF_GROUP // per_down):
            acc_ref[...] += jnp.dot(act_ref[p], wdn_ref[g * (FF_GROUP // per_down) + p],
                                    preferred_element_type=F32)
        return carry

    lax.fori_loop(0, n_up // FF_GROUP, body, 0)
    out_ref[0] = out_ref[0] + _rms(acc_ref[...], gpo_ref[...])


def _ffn(x, oa, ob, gb, wo, gpm, gpf, wup, cw, cb, wdn, gpo):
    bsz, seq, d_model = x.shape
    tile = FFN_TILE
    tiles_per_seq = seq // tile
    n_up, ff_block = wup.shape[1], wup.shape[3]
    const = lambda *shape: pl.BlockSpec(shape, lambda i: (0,) * len(shape),
                                        pipeline_mode=pl.Buffered(1))
    tok = lambda width: pl.BlockSpec(
        (1, tile, width), lambda i: (i // tiles_per_seq, i % tiles_per_seq, 0))
    return pl.pallas_call(
        functools.partial(_ffn_kernel, tiles_per_seq=tiles_per_seq),
        grid=(bsz * tiles_per_seq,),
        in_specs=[tok(d_model), tok(oa.shape[-1]), tok(ob.shape[-1]), const(*gb.shape),
                  const(*wo.shape), const(*gpm.shape), const(*gpf.shape), const(*wup.shape),
                  const(*cw.shape), const(*cb.shape), const(*wdn.shape), const(*gpo.shape)],
        out_specs=tok(d_model),
        out_shape=jax.ShapeDtypeStruct(x.shape, F32),
        scratch_shapes=[pltpu.VMEM((tile, d_model), BF16),
                        pltpu.VMEM((FF_GROUP, 2, tile + SUBLANES, ff_block), F32),
                        pltpu.VMEM((FF_GROUP * ff_block // wdn.shape[1], tile, wdn.shape[1]), BF16),
                        pltpu.VMEM((2, n_up, SUBLANES, ff_block), F32),
                        pltpu.VMEM((tile, d_model), F32)],
        compiler_params=pltpu.CompilerParams(
            dimension_semantics=("arbitrary",), vmem_limit_bytes=VMEM_LIMIT_BYTES),
        name="ffn",
    )(x, oa, ob, gb, wo, gpm, gpf, wup, cw, cb, wdn, gpo)


def _rope_tables(seq):
    half = ROT_DIM // 2
    inv = ROPE_THETA ** (-jnp.arange(0, ROT_DIM, 2, dtype=F32) / ROT_DIM)
    ang = jnp.arange(seq, dtype=F32)[:, None] * inv[None, :]
    cos, sin = jnp.cos(ang), jnp.sin(ang)
    ones = jnp.ones((seq, HEAD_DIM - ROT_DIM), F32)
    zeros = jnp.zeros((seq, HEAD_DIM - ROT_DIM), F32)
    zh = jnp.zeros((seq, half), F32)
    per_head = lambda parts: jnp.tile(jnp.concatenate(parts, axis=1), (1, LANES // HEAD_DIM))
    return (per_head([cos, cos, ones]), per_head([-sin, zh, zeros]), per_head([zh, sin, zeros]))


def kernel(x, pre_mix_norm, w_in, v_norm_g, v_norm_b, w_spatial, b_spatial, out_norm_a, out_norm_b, w_out, post_mix_norm, pre_ffn_norm, w_up, conv_w, conv_b, w_down, post_ffn_norm):
    depth = w_in.shape[0]
    seq = x.shape[1]
    d_ff = w_down.shape[1]
    n_ff = d_ff // FF_BLOCK
    cos_t, sa_t, sb_t = _rope_tables(seq)
    row = lambda p: p[None, :]
    for l in range(depth):
        wup = w_up[l].astype(BF16).reshape(-1, 2, n_ff, FF_BLOCK).transpose(1, 2, 0, 3)
        cw = conv_w[l].reshape(CONV_WIDTH, 2, n_ff, FF_BLOCK).transpose(1, 2, 0, 3)
        cb = conv_b[l].reshape(2, n_ff, 1, FF_BLOCK)
        wdn = w_down[l].astype(BF16).reshape(d_ff // FF_DOWN_BLOCK, FF_DOWN_BLOCK, -1)

        oa, q, k, v = _in_proj(x, row(pre_mix_norm[l]), w_in[l].astype(BF16), row(v_norm_g[l]),
                               row(v_norm_b[l]), w_spatial[l], b_spatial[l].T,
                               row(out_norm_a[l]), cos_t, sa_t, sb_t)
        ob = _attn(q, k, v)
        x = _ffn(x, oa, ob, row(out_norm_b[l]), w_out[l].astype(BF16), row(post_mix_norm[l]),
                 row(pre_ffn_norm[l]), wup, cw, cb, wdn, row(post_ffn_norm[l]))
    return x
```

```python
import functools

import numpy as np
import jax
import jax.numpy as jnp
from jax import lax
from jax.experimental import pallas as pl
from jax.experimental.pallas import tpu as pltpu

F32 = jnp.float32
BF16 = jnp.bfloat16

A_GROUPS = 4
CHUNK = 128
HEAD_DIM = 64
ROT_DIM = 16
ROPE_THETA = 500000.0
DILATIONS = (1, 4, 16)
BAND = 128
CONV_WIDTH = 3
EPS = 1e-6
NEG_INF = -1e30

LANES = 128
SUBLANES = 8
VMEM_LIMIT_BYTES = 56 * 1024 * 1024

IN_TILE = 512
FFN_TILE = 512
FF_BLOCK = 256
FF_DOWN_BLOCK = 512
FF_GROUP = 8
ATTN_UNROLL = 8


def _rms(x, g):
    return x * lax.rsqrt(jnp.mean(x * x, axis=-1, keepdims=True) + EPS) * g


def _in_proj_kernel(x_ref, g_ref, w_ref, vg_ref, vb_ref, ws_ref, bs_ref, ga_ref,
                    cos_ref, sa_ref, sb_ref, oa_ref, q_ref, k_ref, v_ref):
    a_width = oa_ref.shape[-1]
    b_width = q_ref.shape[-1]
    tile = x_ref.shape[1]

    h = _rms(x_ref[0], g_ref[...]).astype(BF16)
    proj = jnp.dot(h, w_ref[...], preferred_element_type=F32)

    za = proj[:, :2 * a_width]
    za = 0.5 * za * (1.0 + lax.erf(za * np.float32(np.sqrt(0.5))))
    u, va = za[:, :a_width], za[:, a_width:]
    mu = jnp.mean(va, axis=-1, keepdims=True)
    vc = va - mu
    va = vc * lax.rsqrt(jnp.mean(vc * vc, axis=-1, keepdims=True) + EPS) * vg_ref[...] + vb_ref[...]
    va = va.astype(BF16)

    row = lax.broadcasted_iota(jnp.int32, (CHUNK, CHUNK), 0)
    col = lax.broadcasted_iota(jnp.int32, (CHUNK, CHUNK), 1)
    causal = col <= row
    cols = []
    for g in range(A_GROUPS):
        wsg = jnp.where(causal, ws_ref[g], 0.0).astype(BF16)
        bias = bs_ref[:, g:g + 1]
        lo = g * CHUNK
        rows = []
        for n in range(tile // CHUNK):
            vblk = va[n * CHUNK:(n + 1) * CHUNK, lo:lo + CHUNK]
            rows.append(jnp.dot(wsg, vblk, preferred_element_type=F32) + bias)
        cols.append(jnp.concatenate(rows, axis=0))
    o_a = u * jnp.concatenate(cols, axis=1)
    oa_ref[0] = _rms(o_a, ga_ref[...]).astype(oa_ref.dtype)

    cos, sa, sb = cos_ref[...], sa_ref[...], sb_ref[...]
    half = ROT_DIM // 2

    def rope(t):
        out = []
        for c in range(b_width // LANES):
            tc = t[:, c * LANES:(c + 1) * LANES]
            out.append(tc * cos + pltpu.roll(tc, LANES - half, axis=1) * sa
                       + pltpu.roll(tc, half, axis=1) * sb)
        return jnp.concatenate(out, axis=1)

    base = 2 * a_width
    q_ref[0] = rope(proj[:, base:base + b_width]) * np.float32(HEAD_DIM ** -0.5)
    k_ref[0] = rope(proj[:, base + b_width:base + 2 * b_width])
    v_ref[0] = proj[:, base + 2 * b_width:]


def _in_proj(x, g, w, vg, vb, ws, bs_t, ga, cos_t, sa_t, sb_t):
    bsz, seq, d_model = x.shape
    a_width = vg.shape[-1]
    b_width = (w.shape[1] - 2 * a_width) // 3
    tile = IN_TILE
    const = lambda *shape: pl.BlockSpec(shape, lambda b, t: (0,) * len(shape),
                                        pipeline_mode=pl.Buffered(1))
    tok = lambda width: pl.BlockSpec((1, tile, width), lambda b, t: (b, t, 0))
    pos = pl.BlockSpec((tile, LANES), lambda b, t: (t, 0))
    return pl.pallas_call(
        _in_proj_kernel,
        grid=(bsz, seq // tile),
        in_specs=[tok(d_model), const(1, d_model), const(*w.shape), const(1, a_width),
                  const(1, a_width), const(*ws.shape), const(*bs_t.shape), const(1, a_width),
                  pos, pos, pos],
        out_specs=[tok(a_width), tok(b_width), tok(b_width), tok(b_width)],
        out_shape=[jax.ShapeDtypeStruct((bsz, seq, a_width), BF16),
                   jax.ShapeDtypeStruct((bsz, seq, b_width), F32),
                   jax.ShapeDtypeStruct((bsz, seq, b_width), F32),
                   jax.ShapeDtypeStruct((bsz, seq, b_width), F32)],
        compiler_params=pltpu.CompilerParams(
            dimension_semantics=("parallel", "parallel"), vmem_limit_bytes=VMEM_LIMIT_BYTES),
        name="in_proj",
    )(x, g, w, vg, vb, ws, bs_t, ga, cos_t, sa_t, sb_t)


def _attn_kernel(q_ref, k_ref, v_ref, o_ref, acc_ref, m_ref, l_ref, bias_ref):
    seq = q_ref.shape[1]
    lane = lax.broadcasted_iota(jnp.int32, (BAND, LANES), 1)
    head0 = lane < HEAD_DIM
    qi = lax.broadcasted_iota(jnp.int32, (BAND, 2 * BAND), 0)
    kj = lax.broadcasted_iota(jnp.int32, (BAND, 2 * BAND), 1)
    dist = qi + BAND - kj
    bias_ref[...] = jnp.where((dist >= 0) & (dist <= BAND), 0.0, NEG_INF).astype(F32)
    ones = jnp.ones((BAND, LANES), BF16)

    def tile(rows_q, rows_prev):
        qb = q_ref[0, rows_q, :].astype(BF16)
        zero = jnp.zeros_like(qb)
        q2 = jnp.concatenate([jnp.where(head0, qb, zero), jnp.where(head0, zero, qb)], axis=0)
        if rows_prev is None:
            kk = k_ref[0, rows_q, :].astype(BF16)
            vv = jnp.concatenate([v_ref[0, rows_q, :].astype(BF16), ones], axis=1)
            bias = bias_ref[:, BAND:]
        else:
            kk = jnp.concatenate([k_ref[0, rows_prev, :], k_ref[0, rows_q, :]], axis=0).astype(BF16)
            vv = jnp.concatenate([v_ref[0, rows_prev, :], v_ref[0, rows_q, :]], axis=0).astype(BF16)
            vv = jnp.concatenate([vv, jnp.concatenate([ones, ones], axis=0)], axis=1)
            bias = bias_ref[...]
        s = lax.dot_general(q2, kk, (((1,), (1,)), ((), ())), preferred_element_type=F32)
        s = s + jnp.concatenate([bias, bias], axis=0)
        m = jnp.max(s, axis=-1, keepdims=True)
        p = jnp.exp(s - m)
        o2 = jnp.dot(p.astype(BF16), vv, preferred_element_type=F32)
        acc = jnp.where(head0, o2[:BAND, :LANES], o2[BAND:, :LANES])
        lf = jnp.where(head0, o2[:BAND, LANES:], o2[BAND:, LANES:])
        mf = jnp.where(head0, jnp.broadcast_to(m[:BAND], (BAND, LANES)),
                       jnp.broadcast_to(m[BAND:], (BAND, LANES)))
        return acc, mf, lf

    def run_tiles(n_tiles, tile_fn, start=0):
        unroll = max(u for u in range(1, ATTN_UNROLL + 1) if (n_tiles - start) % u == 0)
        if n_tiles - start == unroll:
            for u in range(unroll):
                tile_fn(start + u)
            return

        def body(i, carry):
            for u in range(unroll):
                tile_fn(start + i * unroll + u)
            return carry

        lax.fori_loop(0, (n_tiles - start) // unroll, body, 0)

    for slot, d in enumerate(DILATIONS[1:]):
        n_blk = seq // (d * BAND)

        def dilated(start, first, slot=slot, d=d):
            rows_q = pl.ds(start, BAND, stride=d)
            rows_prev = None if first else pl.ds(start - BAND * d, BAND, stride=d)
            acc, mf, lf = tile(rows_q, rows_prev)
            acc_ref[slot, rows_q, :] = acc
            m_ref[slot, rows_q, :] = mf
            l_ref[slot, rows_q, :] = lf

        run_tiles(d, lambda t, dilated=dilated: dilated(t, True))
        run_tiles(d * (n_blk - 1),
                  lambda t, dilated=dilated, d=d: dilated((1 + t // d) * (BAND * d) + t % d, False))

    def merge(n):
        static = isinstance(n, int)
        start = n * BAND if static else pl.multiple_of(n * BAND, BAND)
        rows = pl.ds(start, BAND)
        rows_prev = None if (static and n == 0) else pl.ds(start - BAND, BAND)
        acc1, m1, l1 = tile(rows, rows_prev)
        m2, m3 = m_ref[0, rows, :], m_ref[1, rows, :]
        m = jnp.maximum(jnp.maximum(m1, m2), m3)
        w1, w2, w3 = jnp.exp(m1 - m), jnp.exp(m2 - m), jnp.exp(m3 - m)
        num = w1 * acc1 + w2 * acc_ref[0, rows, :] + w3 * acc_ref[1, rows, :]
        den = w1 * l1 + w2 * l_ref[0, rows, :] + w3 * l_ref[1, rows, :]
        o_ref[0, rows, :] = num / den

    run_tiles(ATTN_UNROLL, merge)
    run_tiles(seq // BAND, merge, start=ATTN_UNROLL)


def _attn(q, k, v):
    bsz, seq, b_width = q.shape
    spec = pl.BlockSpec((1, seq, LANES), lambda b, c: (b, 0, c))
    n_dilated = len(DILATIONS) - 1
    return pl.pallas_call(
        _attn_kernel,
        grid=(bsz, b_width // LANES),
        in_specs=[spec, spec, spec],
        out_specs=spec,
        out_shape=jax.ShapeDtypeStruct((bsz, seq, b_width), F32),
        scratch_shapes=[pltpu.VMEM((n_dilated, seq, LANES), F32)] * 3
        + [pltpu.VMEM((BAND, 2 * BAND), F32)],
        compiler_params=pltpu.CompilerParams(
            dimension_semantics=("parallel", "parallel"), vmem_limit_bytes=VMEM_LIMIT_BYTES),
        name="attn",
    )(q, k, v)


def _ffn_kernel(x_ref, oa_ref, ob_ref, gb_ref, wo_ref, gpm_ref, gpf_ref, wup_ref, cw_ref,
                cb_ref, wdn_ref, gpo_ref, out_ref, h_ref, up_ref, act_ref, carry_ref, acc_ref,
                *, tiles_per_seq):
    tile = x_ref.shape[1]
    d_ff = wdn_ref.shape[0]
    ff_block, n_up = FF_BLOCK, d_ff // FF_BLOCK
    per_down = FF_DOWN_BLOCK // ff_block

    def cols(part, j):
        return pl.ds(pl.multiple_of(part * d_ff + j * ff_block, ff_block), ff_block)
    first_tile = pl.program_id(0) % tiles_per_seq == 0

    mixed = jnp.concatenate([oa_ref[0], _rms(ob_ref[0], gb_ref[...]).astype(BF16)], axis=1)
    y = jnp.dot(mixed, wo_ref[...], preferred_element_type=F32)
    x1 = x_ref[0] + _rms(y, gpm_ref[...])
    out_ref[0] = x1
    h_ref[...] = _rms(x1, gpf_ref[...]).astype(BF16)
    acc_ref[...] = jnp.zeros_like(acc_ref)

    def up_proj(j, sub):
        h = h_ref[...]
        for part in range(2):
            up = jnp.dot(h, wup_ref[:, cols(part, j)], preferred_element_type=F32)
            buf = up_ref.at[sub, part]
            prev = carry_ref[part, j]
            buf[pl.ds(0, SUBLANES), :] = jnp.where(first_tile, jnp.zeros_like(prev), prev)
            buf[pl.ds(SUBLANES, tile), :] = up
            carry_ref[part, j] = up[tile - SUBLANES:]

    def activate(j, sub):
        conv = []
        for part in range(2):
            buf = up_ref.at[sub, part]
            w = cw_ref[:, cols(part, j)]
            out = cb_ref[:, cols(part, j)]
            for i in range(CONV_WIDTH):
                shift = SUBLANES - (CONV_WIDTH - 1) + i
                out = out + w[i:i + 1] * buf[pl.ds(shift, tile), :]
            conv.append(out)
        gate, val = conv
        cdf = 0.5 * (1.0 + jnp.tanh(np.float32(np.sqrt(2.0 / np.pi))
                                    * (gate + np.float32(0.044715) * (gate * gate * gate))))
        lo = (sub % per_down) * ff_block
        act_ref[sub // per_down, :, lo:lo + ff_block] = (gate * cdf * val).astype(BF16)

    def body(g, carry):
        for sub in range(FF_GROUP):
            up_proj(g * FF_GROUP + sub, sub)
        for sub in range(FF_GROUP):
            activate(g * FF_GROUP + sub, sub)
        for p in range(FF_GROUP // per_down):
            rows = pl.ds(pl.multiple_of((g * (FF_GROUP // per_down) + p) * FF_DOWN_BLOCK,
                                        FF_DOWN_BLOCK), FF_DOWN_BLOCK)
            acc_ref[...] += jnp.dot(act_ref[p], wdn_ref[rows, :], preferred_element_type=F32)
        return carry

    lax.fori_loop(0, n_up // FF_GROUP, body, 0)
    out_ref[0] = out_ref[0] + _rms(acc_ref[...], gpo_ref[...])


def _ffn(x, oa, ob, gb, wo, gpm, gpf, wup, cw, cb, wdn, gpo):
    bsz, seq, d_model = x.shape
    tile = FFN_TILE
    tiles_per_seq = seq // tile
    ff_block, n_up = FF_BLOCK, wdn.shape[0] // FF_BLOCK
    const = lambda *shape: pl.BlockSpec(shape, lambda i: (0,) * len(shape),
                                        pipeline_mode=pl.Buffered(1))
    tok = lambda width: pl.BlockSpec(
        (1, tile, width), lambda i: (i // tiles_per_seq, i % tiles_per_seq, 0))
    return pl.pallas_call(
        functools.partial(_ffn_kernel, tiles_per_seq=tiles_per_seq),
        grid=(bsz * tiles_per_seq,),
        in_specs=[tok(d_model), tok(oa.shape[-1]), tok(ob.shape[-1]), const(*gb.shape),
                  const(*wo.shape), const(*gpm.shape), const(*gpf.shape), const(*wup.shape),
                  const(*cw.shape), const(*cb.shape), const(*wdn.shape), const(*gpo.shape)],
        out_specs=tok(d_model),
        out_shape=jax.ShapeDtypeStruct(x.shape, F32),
        scratch_shapes=[pltpu.VMEM((tile, d_model), BF16),
                        pltpu.VMEM((FF_GROUP, 2, tile + SUBLANES, ff_block), F32),
                        pltpu.VMEM((FF_GROUP * ff_block // FF_DOWN_BLOCK, tile, FF_DOWN_BLOCK), BF16),
                        pltpu.VMEM((2, n_up, SUBLANES, ff_block), F32),
                        pltpu.VMEM((tile, d_model), F32)],
        compiler_params=pltpu.CompilerParams(
            dimension_semantics=("arbitrary",), vmem_limit_bytes=VMEM_LIMIT_BYTES),
        name="ffn",
    )(x, oa, ob, gb, wo, gpm, gpf, wup, cw, cb, wdn, gpo)


def _rope_tables(seq):
    half = ROT_DIM // 2
    inv = ROPE_THETA ** (-jnp.arange(0, ROT_DIM, 2, dtype=F32) / ROT_DIM)
    ang = jnp.arange(seq, dtype=F32)[:, None] * inv[None, :]
    cos, sin = jnp.cos(ang), jnp.sin(ang)
    ones = jnp.ones((seq, HEAD_DIM - ROT_DIM), F32)
    zeros = jnp.zeros((seq, HEAD_DIM - ROT_DIM), F32)
    zh = jnp.zeros((seq, half), F32)
    per_head = lambda parts: jnp.tile(jnp.concatenate(parts, axis=1), (1, LANES // HEAD_DIM))
    return (per_head([cos, cos, ones]), per_head([-sin, zh, zeros]), per_head([zh, sin, zeros]))


def kernel(x, pre_mix_norm, w_in, v_norm_g, v_norm_b, w_spatial, b_spatial, out_norm_a, out_norm_b, w_out, post_mix_norm, pre_ffn_norm, w_up, conv_w, conv_b, w_down, post_ffn_norm):
    depth = w_in.shape[0]
    seq = x.shape[1]
    cos_t, sa_t, sb_t = _rope_tables(seq)
    row = lambda p: p[None, :]
    for l in range(depth):
        oa, q, k, v = _in_proj(x, row(pre_mix_norm[l]), w_in[l].astype(BF16), row(v_norm_g[l]),
                               row(v_norm_b[l]), w_spatial[l], b_spatial[l].T,
                               row(out_norm_a[l]), cos_t, sa_t, sb_t)
        ob = _attn(q, k, v)
        x = _ffn(x, oa, ob, row(out_norm_b[l]), w_out[l].astype(BF16), row(post_mix_norm[l]),
                 row(pre_ffn_norm[l]), w_up[l].astype(BF16), conv_w[l], row(conv_b[l]),
                 w_down[l].astype(BF16), row(post_ffn_norm[l]))
    return x
```

```python
import functools

import numpy as np
import jax
import jax.numpy as jnp
from jax import lax
from jax.experimental import pallas as pl
from jax.experimental.pallas import tpu as pltpu

F32 = jnp.float32
BF16 = jnp.bfloat16

A_GROUPS = 4
CHUNK = 128
HEAD_DIM = 64
ROT_DIM = 16
ROPE_THETA = 500000.0
DILATIONS = (1, 4, 16)
BAND = 128
CONV_WIDTH = 3
EPS = 1e-6
NEG_INF = -1e30

LANES = 128
SUBLANES = 8
VMEM_LIMIT_BYTES = 56 * 1024 * 1024

IN_TILE = 512
FFN_TILE = 512
FF_BLOCK = 256
FF_DOWN_BLOCK = 512
FF_GROUP = 8
ATTN_UNROLL = 8


def _rms(x, g):
    return x * lax.rsqrt(jnp.mean(x * x, axis=-1, keepdims=True) + EPS) * g


def _in_proj_kernel(x_ref, g_ref, w_ref, vg_ref, vb_ref, ws_ref, bs_ref, ga_ref,
                    cos_ref, sa_ref, sb_ref, oa_ref, q_ref, k_ref, v_ref):
    a_width = oa_ref.shape[-1]
    b_width = q_ref.shape[-1]
    tile = x_ref.shape[1]

    h = _rms(x_ref[0], g_ref[...]).astype(BF16)
    proj = jnp.dot(h, w_ref[...], preferred_element_type=F32)

    za = proj[:, :2 * a_width]
    za = 0.5 * za * (1.0 + lax.erf(za * np.float32(np.sqrt(0.5))))
    u, va = za[:, :a_width], za[:, a_width:]
    mu = jnp.mean(va, axis=-1, keepdims=True)
    vc = va - mu
    va = vc * lax.rsqrt(jnp.mean(vc * vc, axis=-1, keepdims=True) + EPS) * vg_ref[...] + vb_ref[...]
    va = va.astype(BF16)

    row = lax.broadcasted_iota(jnp.int32, (CHUNK, CHUNK), 0)
    col = lax.broadcasted_iota(jnp.int32, (CHUNK, CHUNK), 1)
    causal = col <= row
    cols = []
    for g in range(A_GROUPS):
        wsg = jnp.where(causal, ws_ref[g], 0.0).astype(BF16)
        bias = bs_ref[:, g:g + 1]
        lo = g * CHUNK
        rows = []
        for n in range(tile // CHUNK):
            vblk = va[n * CHUNK:(n + 1) * CHUNK, lo:lo + CHUNK]
            rows.append(jnp.dot(wsg, vblk, preferred_element_type=F32) + bias)
        cols.append(jnp.concatenate(rows, axis=0))
    o_a = u * jnp.concatenate(cols, axis=1)
    oa_ref[0] = _rms(o_a, ga_ref[...]).astype(oa_ref.dtype)

    cos, sa, sb = cos_ref[...], sa_ref[...], sb_ref[...]
    half = ROT_DIM // 2

    def rope(t):
        out = []
        for c in range(b_width // LANES):
            tc = t[:, c * LANES:(c + 1) * LANES]
            out.append(tc * cos + pltpu.roll(tc, LANES - half, axis=1) * sa
                       + pltpu.roll(tc, half, axis=1) * sb)
        return jnp.concatenate(out, axis=1)

    base = 2 * a_width
    q_ref[0] = rope(proj[:, base:base + b_width]) * np.float32(HEAD_DIM ** -0.5)
    k_ref[0] = rope(proj[:, base + b_width:base + 2 * b_width])
    v_ref[0] = proj[:, base + 2 * b_width:]


def _in_proj(x, g, w, vg, vb, ws, bs_t, ga, cos_t, sa_t, sb_t):
    bsz, seq, d_model = x.shape
    a_width = vg.shape[-1]
    b_width = (w.shape[1] - 2 * a_width) // 3
    tile = IN_TILE
    const = lambda *shape: pl.BlockSpec(shape, lambda b, t: (0,) * len(shape),
                                        pipeline_mode=pl.Buffered(1))
    tok = lambda width: pl.BlockSpec((1, tile, width), lambda b, t: (b, t, 0))
    pos = pl.BlockSpec((tile, LANES), lambda b, t: (t, 0))
    return pl.pallas_call(
        _in_proj_kernel,
        grid=(bsz, seq // tile),
        in_specs=[tok(d_model), const(1, d_model), const(*w.shape), const(1, a_width),
                  const(1, a_width), const(*ws.shape), const(*bs_t.shape), const(1, a_width),
                  pos, pos, pos],
        out_specs=[tok(a_width), tok(b_width), tok(b_width), tok(b_width)],
        out_shape=[jax.ShapeDtypeStruct((bsz, seq, a_width), BF16),
                   jax.ShapeDtypeStruct((bsz, seq, b_width), F32),
                   jax.ShapeDtypeStruct((bsz, seq, b_width), F32),
                   jax.ShapeDtypeStruct((bsz, seq, b_width), F32)],
        compiler_params=pltpu.CompilerParams(
            dimension_semantics=("parallel", "parallel"), vmem_limit_bytes=VMEM_LIMIT_BYTES),
        name="in_proj",
    )(x, g, w, vg, vb, ws, bs_t, ga, cos_t, sa_t, sb_t)


def _attn_kernel(q_ref, k_ref, v_ref, o_ref, acc_ref, m_ref, l_ref, bias_ref):
    seq = q_ref.shape[1]
    lane = lax.broadcasted_iota(jnp.int32, (BAND, LANES), 1)
    head0 = lane < HEAD_DIM
    qi = lax.broadcasted_iota(jnp.int32, (BAND, 2 * BAND), 0)
    kj = lax.broadcasted_iota(jnp.int32, (BAND, 2 * BAND), 1)
    dist = qi + BAND - kj
    bias_ref[...] = jnp.where((dist >= 0) & (dist <= BAND), 0.0, NEG_INF).astype(F32)
    ones = jnp.ones((BAND, LANES), BF16)

    def tile(rows_q, rows_prev):
        qb = q_ref[0, rows_q, :].astype(BF16)
        zero = jnp.zeros_like(qb)
        q2 = jnp.concatenate([jnp.where(head0, qb, zero), jnp.where(head0, zero, qb)], axis=0)
        if rows_prev is None:
            kk = k_ref[0, rows_q, :].astype(BF16)
            vv = jnp.concatenate([v_ref[0, rows_q, :].astype(BF16), ones], axis=1)
            bias = bias_ref[:, BAND:]
        else:
            kk = jnp.concatenate([k_ref[0, rows_prev, :], k_ref[0, rows_q, :]], axis=0).astype(BF16)
            vv = jnp.concatenate([v_ref[0, rows_prev, :], v_ref[0, rows_q, :]], axis=0).astype(BF16)
            vv = jnp.concatenate([vv, jnp.concatenate([ones, ones], axis=0)], axis=1)
            bias = bias_ref[...]
        s = lax.dot_general(q2, kk, (((1,), (1,)), ((), ())), preferred_element_type=F32)
        yield
        s = s + jnp.concatenate([bias, bias], axis=0)
        m = jnp.max(s, axis=-1, keepdims=True)
        p = jnp.exp(s - m)
        yield
        o2 = jnp.dot(p.astype(BF16), vv, preferred_element_type=F32)
        yield
        acc = jnp.where(head0, o2[:BAND, :LANES], o2[BAND:, :LANES])
        lf = jnp.where(head0, o2[:BAND, LANES:], o2[BAND:, LANES:])
        mf = jnp.where(head0, jnp.broadcast_to(m[:BAND], (BAND, LANES)),
                       jnp.broadcast_to(m[BAND:], (BAND, LANES)))
        return acc, mf, lf

    def run_tiles(n_tiles, tile_fn, start=0, cap=ATTN_UNROLL, staged=True):
        unroll = max(u for u in range(1, cap + 1) if (n_tiles - start) % u == 0)

        def in_step(tiles):
            running = [tile_fn(t) for t in tiles]
            while running:
                if staged:
                    running = [g for g in running if next(g, StopIteration) is not StopIteration]
                else:
                    for _ in running.pop(0):
                        pass

        if n_tiles - start == unroll:
            in_step([start + u for u in range(unroll)])
            return

        def body(i, carry):
            in_step([start + i * unroll + u for u in range(unroll)])
            return carry

        lax.fori_loop(0, (n_tiles - start) // unroll, body, 0)

    for slot, d in enumerate(DILATIONS[1:]):
        n_blk = seq // (d * BAND)

        def dilated(start, first, slot=slot, d=d):
            rows_q = pl.ds(start, BAND, stride=d)
            rows_prev = None if first else pl.ds(start - BAND * d, BAND, stride=d)
            acc, mf, lf = yield from tile(rows_q, rows_prev)
            acc_ref[slot, rows_q, :] = acc
            m_ref[slot, rows_q, :] = mf
            l_ref[slot, rows_q, :] = lf

        run_tiles(d, lambda t, dilated=dilated: dilated(t, True))
        run_tiles(d * (n_blk - 1),
                  lambda t, dilated=dilated, d=d: dilated((1 + t // d) * (BAND * d) + t % d, False),
                  staged=d <= SUBLANES)

    def merge(n):
        static = isinstance(n, int)
        start = n * BAND if static else pl.multiple_of(n * BAND, BAND)
        rows = pl.ds(start, BAND)
        rows_prev = None if (static and n == 0) else pl.ds(start - BAND, BAND)
        acc1, m1, l1 = yield from tile(rows, rows_prev)
        m2, m3 = m_ref[0, rows, :], m_ref[1, rows, :]
        m = jnp.maximum(jnp.maximum(m1, m2), m3)
        w1, w2, w3 = jnp.exp(m1 - m), jnp.exp(m2 - m), jnp.exp(m3 - m)
        num = w1 * acc1 + w2 * acc_ref[0, rows, :] + w3 * acc_ref[1, rows, :]
        den = w1 * l1 + w2 * l_ref[0, rows, :] + w3 * l_ref[1, rows, :]
        o_ref[0, rows, :] = num / den

    run_tiles(ATTN_UNROLL, merge)
    run_tiles(seq // BAND, merge, start=ATTN_UNROLL)


def _attn(q, k, v):
    bsz, seq, b_width = q.shape
    spec = pl.BlockSpec((1, seq, LANES), lambda b, c: (b, 0, c))
    n_dilated = len(DILATIONS) - 1
    return pl.pallas_call(
        _attn_kernel,
        grid=(bsz, b_width // LANES),
        in_specs=[spec, spec, spec],
        out_specs=spec,
        out_shape=jax.ShapeDtypeStruct((bsz, seq, b_width), F32),
        scratch_shapes=[pltpu.VMEM((n_dilated, seq, LANES), F32)] * 3
        + [pltpu.VMEM((BAND, 2 * BAND), F32)],
        compiler_params=pltpu.CompilerParams(
            dimension_semantics=("parallel", "parallel"), vmem_limit_bytes=VMEM_LIMIT_BYTES),
        name="attn",
    )(q, k, v)


def _ffn_kernel(x_ref, oa_ref, ob_ref, gb_ref, wo_ref, gpm_ref, gpf_ref, wup_ref, cw_ref,
                cb_ref, wdn_ref, gpo_ref, out_ref, h_ref, up_ref, act_ref, carry_ref, acc_ref,
                *, tiles_per_seq):
    tile = x_ref.shape[1]
    d_ff = wdn_ref.shape[0]
    ff_block, n_up = FF_BLOCK, d_ff // FF_BLOCK
    per_down = FF_DOWN_BLOCK // ff_block

    def cols(part, j):
        return pl.ds(pl.multiple_of(part * d_ff + j * ff_block, ff_block), ff_block)
    first_tile = pl.program_id(0) % tiles_per_seq == 0

    mixed = jnp.concatenate([oa_ref[0], _rms(ob_ref[0], gb_ref[...]).astype(BF16)], axis=1)
    y = jnp.dot(mixed, wo_ref[...], preferred_element_type=F32)
    x1 = x_ref[0] + _rms(y, gpm_ref[...])
    out_ref[0] = x1
    h_ref[...] = _rms(x1, gpf_ref[...]).astype(BF16)
    acc_ref[...] = jnp.zeros_like(acc_ref)

    def up_proj(j, sub):
        h = h_ref[...]
        for part in range(2):
            up = jnp.dot(h, wup_ref[:, cols(part, j)], preferred_element_type=F32)
            buf = up_ref.at[sub, part]
            prev = carry_ref[part, j]
            buf[pl.ds(0, SUBLANES), :] = jnp.where(first_tile, jnp.zeros_like(prev), prev)
            buf[pl.ds(SUBLANES, tile), :] = up
            carry_ref[part, j] = up[tile - SUBLANES:]

    def activate(j, sub):
        conv = []
        for part in range(2):
            buf = up_ref.at[sub, part]
            w = cw_ref[:, cols(part, j)]
            out = cb_ref[:, cols(part, j)]
            for i in range(CONV_WIDTH):
                shift = SUBLANES - (CONV_WIDTH - 1) + i
                out = out + w[i:i + 1] * buf[pl.ds(shift, tile), :]
            conv.append(out)
        gate, val = conv
        cdf = 0.5 * (1.0 + jnp.tanh(np.float32(np.sqrt(2.0 / np.pi))
                                    * (gate + np.float32(0.044715) * (gate * gate * gate))))
        lo = (sub % per_down) * ff_block
        act_ref[sub // per_down, :, lo:lo + ff_block] = (gate * cdf * val).astype(BF16)

    def body(g, carry):
        for sub in range(FF_GROUP):
            up_proj(g * FF_GROUP + sub, sub)
        for sub in range(FF_GROUP):
            activate(g * FF_GROUP + sub, sub)
        for p in range(FF_GROUP // per_down):
            rows = pl.ds(pl.multiple_of((g * (FF_GROUP // per_down) + p) * FF_DOWN_BLOCK,
                                        FF_DOWN_BLOCK), FF_DOWN_BLOCK)
            acc_ref[...] += jnp.dot(act_ref[p], wdn_ref[rows, :], preferred_element_type=F32)
        return carry

    lax.fori_loop(0, n_up // FF_GROUP, body, 0)
    out_ref[0] = out_ref[0] + _rms(acc_ref[...], gpo_ref[...])


def _ffn(x, oa, ob, gb, wo, gpm, gpf, wup, cw, cb, wdn, gpo):
    bsz, seq, d_model = x.shape
    tile = FFN_TILE
    tiles_per_seq = seq // tile
    ff_block, n_up = FF_BLOCK, wdn.shape[0] // FF_BLOCK
    const = lambda *shape: pl.BlockSpec(shape, lambda i: (0,) * len(shape),
                                        pipeline_mode=pl.Buffered(1))
    tok = lambda width: pl.BlockSpec(
        (1, tile, width), lambda i: (i // tiles_per_seq, i % tiles_per_seq, 0))
    return pl.pallas_call(
        functools.partial(_ffn_kernel, tiles_per_seq=tiles_per_seq),
        grid=(bsz * tiles_per_seq,),
        in_specs=[tok(d_model), tok(oa.shape[-1]), tok(ob.shape[-1]), const(*gb.shape),
                  const(*wo.shape), const(*gpm.shape), const(*gpf.shape), const(*wup.shape),
                  const(*cw.shape), const(*cb.shape), const(*wdn.shape), const(*gpo.shape)],
        out_specs=tok(d_model),
        out_shape=jax.ShapeDtypeStruct(x.shape, F32),
        scratch_shapes=[pltpu.VMEM((tile, d_model), BF16),
                        pltpu.VMEM((FF_GROUP, 2, tile + SUBLANES, ff_block), F32),
                        pltpu.VMEM((FF_GROUP * ff_block // FF_DOWN_BLOCK, tile, FF_DOWN_BLOCK), BF16),
                        pltpu.VMEM((2, n_up, SUBLANES, ff_block), F32),
                        pltpu.VMEM((tile, d_model), F32)],
        compiler_params=pltpu.CompilerParams(
            dimension_semantics=("arbitrary",), vmem_limit_bytes=VMEM_LIMIT_BYTES),
        name="ffn",
    )(x, oa, ob, gb, wo, gpm, gpf, wup, cw, cb, wdn, gpo)


def _rope_tables(seq):
    half = ROT_DIM // 2
    inv = ROPE_THETA ** (-jnp.arange(0, ROT_DIM, 2, dtype=F32) / ROT_DIM)
    ang = jnp.arange(seq, dtype=F32)[:, None] * inv[None, :]
    cos, sin = jnp.cos(ang), jnp.sin(ang)
    ones = jnp.ones((seq, HEAD_DIM - ROT_DIM), F32)
    zeros = jnp.zeros((seq, HEAD_DIM - ROT_DIM), F32)
    zh = jnp.zeros((seq, half), F32)
    per_head = lambda parts: jnp.tile(jnp.concatenate(parts, axis=1), (1, LANES // HEAD_DIM))
    return (per_head([cos, cos, ones]), per_head([-sin, zh, zeros]), per_head([zh, sin, zeros]))


def kernel(x, pre_mix_norm, w_in, v_norm_g, v_norm_b, w_spatial, b_spatial, out_norm_a, out_norm_b, w_out, post_mix_norm, pre_ffn_norm, w_up, conv_w, conv_b, w_down, post_ffn_norm):
    depth = w_in.shape[0]
    seq = x.shape[1]
    cos_t, sa_t, sb_t = _rope_tables(seq)
    row = lambda p: p[None, :]
    for l in range(depth):
        oa, q, k, v = _in_proj(x, row(pre_mix_norm[l]), w_in[l].astype(BF16), row(v_norm_g[l]),
                               row(v_norm_b[l]), w_spatial[l], b_spatial[l].T,
                               row(out_norm_a[l]), cos_t, sa_t, sb_t)
        ob = _attn(q, k, v)
        x = _ffn(x, oa, ob, row(out_norm_b[l]), w_out[l].astype(BF16), row(post_mix_norm[l]),
                 row(pre_ffn_norm[l]), w_up[l].astype(BF16), conv_w[l], row(conv_b[l]),
                 w_down[l].astype(BF16), row(post_ffn_norm[l]))
    return x
```

```python
import functools

import numpy as np
import jax
import jax.numpy as jnp
from jax import lax
from jax.experimental import pallas as pl
from jax.experimental.pallas import tpu as pltpu

F32 = jnp.float32
BF16 = jnp.bfloat16

A_GROUPS = 4
CHUNK = 128
HEAD_DIM = 64
ROT_DIM = 16
ROPE_THETA = 500000.0
DILATIONS = (1, 4, 16)
BAND = 128
CONV_WIDTH = 3
EPS = 1e-6
NEG_INF = -1e30

LANES = 128
SUBLANES = 8
VMEM_LIMIT_BYTES = 56 * 1024 * 1024

IN_TILE = 512
FFN_TILE = 512
FF_BLOCK = 256
FF_DOWN_BLOCK = 512
FF_GROUP = 8
ATTN_UNROLL = 16


def _rms(x, g):
    return x * lax.rsqrt(jnp.mean(x * x, axis=-1, keepdims=True) + EPS) * g


def _in_proj_kernel(x_ref, g_ref, w_ref, vg_ref, vb_ref, ws_ref, bs_ref, ga_ref,
                    cos_ref, sa_ref, sb_ref, oa_ref, q_ref, k_ref, v_ref):
    a_width = oa_ref.shape[-1]
    b_width = q_ref.shape[-1]
    tile = x_ref.shape[1]

    h = _rms(x_ref[0], g_ref[...]).astype(BF16)
    proj = jnp.dot(h, w_ref[...], preferred_element_type=F32)

    za = proj[:, :2 * a_width]
    za = 0.5 * za * (1.0 + lax.erf(za * np.float32(np.sqrt(0.5))))
    u, va = za[:, :a_width], za[:, a_width:]
    mu = jnp.mean(va, axis=-1, keepdims=True)
    vc = va - mu
    va = vc * lax.rsqrt(jnp.mean(vc * vc, axis=-1, keepdims=True) + EPS) * vg_ref[...] + vb_ref[...]
    va = va.astype(BF16)

    row = lax.broadcasted_iota(jnp.int32, (CHUNK, CHUNK), 0)
    col = lax.broadcasted_iota(jnp.int32, (CHUNK, CHUNK), 1)
    causal = col <= row
    cols = []
    for g in range(A_GROUPS):
        wsg = jnp.where(causal, ws_ref[g], 0.0).astype(BF16)
        bias = bs_ref[:, g:g + 1]
        lo = g * CHUNK
        rows = []
        for n in range(tile // CHUNK):
            vblk = va[n * CHUNK:(n + 1) * CHUNK, lo:lo + CHUNK]
            rows.append(jnp.dot(wsg, vblk, preferred_element_type=F32) + bias)
        cols.append(jnp.concatenate(rows, axis=0))
    o_a = u * jnp.concatenate(cols, axis=1)
    oa_ref[0] = _rms(o_a, ga_ref[...]).astype(oa_ref.dtype)

    cos, sa, sb = cos_ref[...], sa_ref[...], sb_ref[...]
    half = ROT_DIM // 2

    def rope(t):
        out = []
        for c in range(b_width // LANES):
            tc = t[:, c * LANES:(c + 1) * LANES]
            out.append(tc * cos + pltpu.roll(tc, LANES - half, axis=1) * sa
                       + pltpu.roll(tc, half, axis=1) * sb)
        return jnp.concatenate(out, axis=1)

    base = 2 * a_width
    q_ref[0] = rope(proj[:, base:base + b_width]) * np.float32(HEAD_DIM ** -0.5)
    k_ref[0] = rope(proj[:, base + b_width:base + 2 * b_width])
    v_ref[0] = proj[:, base + 2 * b_width:]


def _in_proj(x, g, w, vg, vb, ws, bs_t, ga, cos_t, sa_t, sb_t):
    bsz, seq, d_model = x.shape
    a_width = vg.shape[-1]
    b_width = (w.shape[1] - 2 * a_width) // 3
    tile = IN_TILE
    assert seq % tile == 0 and tile % CHUNK == 0 and a_width == A_GROUPS * CHUNK
    assert w.shape == (d_model, 2 * a_width + 3 * b_width) and b_width % LANES == 0
    const = lambda *shape: pl.BlockSpec(shape, lambda b, t: (0,) * len(shape),
                                        pipeline_mode=pl.Buffered(1))
    tok = lambda width: pl.BlockSpec((1, tile, width), lambda b, t: (b, t, 0))
    pos = pl.BlockSpec((tile, LANES), lambda b, t: (t, 0))
    return pl.pallas_call(
        _in_proj_kernel,
        grid=(bsz, seq // tile),
        in_specs=[tok(d_model), const(1, d_model), const(*w.shape), const(1, a_width),
                  const(1, a_width), const(*ws.shape), const(*bs_t.shape), const(1, a_width),
                  pos, pos, pos],
        out_specs=[tok(a_width), tok(b_width), tok(b_width), tok(b_width)],
        out_shape=[jax.ShapeDtypeStruct((bsz, seq, a_width), BF16),
                   jax.ShapeDtypeStruct((bsz, seq, b_width), F32),
                   jax.ShapeDtypeStruct((bsz, seq, b_width), F32),
                   jax.ShapeDtypeStruct((bsz, seq, b_width), F32)],
        compiler_params=pltpu.CompilerParams(
            dimension_semantics=("parallel", "parallel"), vmem_limit_bytes=VMEM_LIMIT_BYTES),
        name="in_proj",
    )(x, g, w, vg, vb, ws, bs_t, ga, cos_t, sa_t, sb_t)


def _attn_kernel(q_ref, k_ref, v_ref, o_ref, acc_ref, m_ref, l_ref, bias_ref):
    seq = q_ref.shape[1]
    lane = lax.broadcasted_iota(jnp.int32, (BAND, LANES), 1)
    head0 = lane < HEAD_DIM
    qi = lax.broadcasted_iota(jnp.int32, (BAND, 2 * BAND), 0)
    kj = lax.broadcasted_iota(jnp.int32, (BAND, 2 * BAND), 1)
    dist = qi + BAND - kj
    bias_ref[...] = jnp.where((dist >= 0) & (dist <= BAND), 0.0, NEG_INF).astype(F32)
    ones = jnp.ones((BAND, LANES), BF16)

    def tile(rows_q, rows_prev):
        qb = q_ref[0, rows_q, :].astype(BF16)
        zero = jnp.zeros_like(qb)
        q2 = jnp.concatenate([jnp.where(head0, qb, zero), jnp.where(head0, zero, qb)], axis=0)
        if rows_prev is None:
            kk = k_ref[0, rows_q, :].astype(BF16)
            vv = jnp.concatenate([v_ref[0, rows_q, :].astype(BF16), ones], axis=1)
            bias = bias_ref[:, BAND:]
        else:
            kk = jnp.concatenate([k_ref[0, rows_prev, :], k_ref[0, rows_q, :]], axis=0).astype(BF16)
            vv = jnp.concatenate([v_ref[0, rows_prev, :], v_ref[0, rows_q, :]], axis=0).astype(BF16)
            vv = jnp.concatenate([vv, jnp.concatenate([ones, ones], axis=0)], axis=1)
            bias = bias_ref[...]
        s = lax.dot_general(q2, kk, (((1,), (1,)), ((), ())), preferred_element_type=F32)
        yield
        s = s + jnp.concatenate([bias, bias], axis=0)
        m = jnp.max(s, axis=-1, keepdims=True)
        p = jnp.exp(s - m)
        yield
        o2 = jnp.dot(p.astype(BF16), vv, preferred_element_type=F32)
        yield
        acc = jnp.where(head0, o2[:BAND, :LANES], o2[BAND:, :LANES])
        lf = jnp.where(head0, o2[:BAND, LANES:], o2[BAND:, LANES:])
        mf = jnp.where(head0, jnp.broadcast_to(m[:BAND], (BAND, LANES)),
                       jnp.broadcast_to(m[BAND:], (BAND, LANES)))
        return acc, mf, lf

    def run_tiles(n_tiles, tile_fn, start=0, cap=ATTN_UNROLL, staged=True):
        unroll = max(u for u in range(1, cap + 1) if (n_tiles - start) % u == 0)

        def in_step(tiles):
            running = [tile_fn(t) for t in tiles]
            while running:
                if staged:
                    running = [g for g in running if next(g, StopIteration) is not StopIteration]
                else:
                    for _ in running.pop(0):
                        pass

        if n_tiles - start == unroll:
            in_step([start + u for u in range(unroll)])
            return

        def body(i, carry):
            in_step([start + i * unroll + u for u in range(unroll)])
            return carry

        lax.fori_loop(0, (n_tiles - start) // unroll, body, 0)

    for slot, d in enumerate(DILATIONS[1:]):
        n_blk = seq // (d * BAND)

        def dilated(start, first, slot=slot, d=d):
            rows_q = pl.ds(start, BAND, stride=d)
            rows_prev = None if first else pl.ds(start - BAND * d, BAND, stride=d)
            acc, mf, lf = yield from tile(rows_q, rows_prev)
            acc_ref[slot, rows_q, :] = acc
            m_ref[slot, rows_q, :] = mf
            l_ref[slot, rows_q, :] = lf

        run_tiles(d, lambda t, dilated=dilated: dilated(t, True))
        run_tiles(d * (n_blk - 1),
                  lambda t, dilated=dilated, d=d: dilated((1 + t // d) * (BAND * d) + t % d, False),
                  staged=d <= SUBLANES)

    def merge(n):
        static = isinstance(n, int)
        start = n * BAND if static else pl.multiple_of(n * BAND, BAND)
        rows = pl.ds(start, BAND)
        rows_prev = None if (static and n == 0) else pl.ds(start - BAND, BAND)
        acc1, m1, l1 = yield from tile(rows, rows_prev)
        m2, m3 = m_ref[0, rows, :], m_ref[1, rows, :]
        m = jnp.maximum(jnp.maximum(m1, m2), m3)
        w1, w2, w3 = jnp.exp(m1 - m), jnp.exp(m2 - m), jnp.exp(m3 - m)
        num = w1 * acc1 + w2 * acc_ref[0, rows, :] + w3 * acc_ref[1, rows, :]
        den = w1 * l1 + w2 * l_ref[0, rows, :] + w3 * l_ref[1, rows, :]
        o_ref[0, rows, :] = num / den

    run_tiles(ATTN_UNROLL, merge)
    run_tiles(seq // BAND, merge, start=ATTN_UNROLL)


def _attn(q, k, v):
    bsz, seq, b_width = q.shape
    assert seq % (max(DILATIONS) * BAND) == 0 and b_width % LANES == 0
    assert LANES == 2 * HEAD_DIM and DILATIONS[0] == 1
    spec = pl.BlockSpec((1, seq, LANES), lambda b, c: (b, 0, c))
    n_dilated = len(DILATIONS) - 1
    return pl.pallas_call(
        _attn_kernel,
        grid=(bsz, b_width // LANES),
        in_specs=[spec, spec, spec],
        out_specs=spec,
        out_shape=jax.ShapeDtypeStruct((bsz, seq, b_width), F32),
        scratch_shapes=[pltpu.VMEM((n_dilated, seq, LANES), F32)] * 3
        + [pltpu.VMEM((BAND, 2 * BAND), F32)],
        compiler_params=pltpu.CompilerParams(
            dimension_semantics=("parallel", "parallel"), vmem_limit_bytes=VMEM_LIMIT_BYTES),
        name="attn",
    )(q, k, v)


def _ffn_kernel(x_ref, oa_ref, ob_ref, gb_ref, wo_ref, gpm_ref, gpf_ref, wup_ref, cw_ref,
                cb_ref, wdn_ref, gpo_ref, out_ref, h_ref, up_ref, act_ref, carry_ref, acc_ref,
                *, tiles_per_seq):
    tile = x_ref.shape[1]
    d_ff = wdn_ref.shape[0]
    ff_block, n_up = FF_BLOCK, d_ff // FF_BLOCK
    per_down = FF_DOWN_BLOCK // ff_block

    def cols(part, j):
        return pl.ds(pl.multiple_of(part * d_ff + j * ff_block, ff_block), ff_block)
    first_tile = pl.program_id(0) % tiles_per_seq == 0

    mixed = jnp.concatenate([oa_ref[0], _rms(ob_ref[0], gb_ref[...]).astype(BF16)], axis=1)
    y = jnp.dot(mixed, wo_ref[...], preferred_element_type=F32)
    x1 = x_ref[0] + _rms(y, gpm_ref[...])
    out_ref[0] = x1
    h_ref[...] = _rms(x1, gpf_ref[...]).astype(BF16)
    acc_ref[...] = jnp.zeros_like(acc_ref)

    def up_proj(j, sub):
        h = h_ref[...]
        for part in range(2):
            up = jnp.dot(h, wup_ref[:, cols(part, j)], preferred_element_type=F32)
            buf = up_ref.at[sub, part]
            prev = carry_ref[part, j]
            buf[pl.ds(0, SUBLANES), :] = jnp.where(first_tile, jnp.zeros_like(prev), prev)
            buf[pl.ds(SUBLANES, tile), :] = up
            carry_ref[part, j] = up[tile - SUBLANES:]

    def activate(j, sub):
        conv = []
        for part in range(2):
            buf = up_ref.at[sub, part]
            w = cw_ref[:, cols(part, j)]
            out = cb_ref[:, cols(part, j)]
            for i in range(CONV_WIDTH):
                shift = SUBLANES - (CONV_WIDTH - 1) + i
                out = out + w[i:i + 1] * buf[pl.ds(shift, tile), :]
            conv.append(out)
        gate, val = conv
        cdf = 0.5 * (1.0 + jnp.tanh(np.float32(np.sqrt(2.0 / np.pi))
                                    * (gate + np.float32(0.044715) * (gate * gate * gate))))
        lo = (sub % per_down) * ff_block
        act_ref[sub // per_down, :, lo:lo + ff_block] = (gate * cdf * val).astype(BF16)

    def body(g, carry):
        for sub in range(FF_GROUP):
            up_proj(g * FF_GROUP + sub, sub)
        for sub in range(FF_GROUP):
            activate(g * FF_GROUP + sub, sub)
        for p in range(FF_GROUP // per_down):
            rows = pl.ds(pl.multiple_of((g * (FF_GROUP // per_down) + p) * FF_DOWN_BLOCK,
                                        FF_DOWN_BLOCK), FF_DOWN_BLOCK)
            acc_ref[...] += jnp.dot(act_ref[p], wdn_ref[rows, :], preferred_element_type=F32)
        return carry

    lax.fori_loop(0, n_up // FF_GROUP, body, 0)
    out_ref[0] = out_ref[0] + _rms(acc_ref[...], gpo_ref[...])


def _ffn(x, oa, ob, gb, wo, gpm, gpf, wup, cw, cb, wdn, gpo):
    bsz, seq, d_model = x.shape
    tile = FFN_TILE
    tiles_per_seq = seq // tile
    ff_block, n_up = FF_BLOCK, wdn.shape[0] // FF_BLOCK
    d_ff = wdn.shape[0]
    assert seq % tile == 0 and wup.shape == (d_model, 2 * d_ff) and cw.shape == (CONV_WIDTH, 2 * d_ff)
    assert d_ff % (FF_GROUP * FF_BLOCK) == 0 and (FF_GROUP * FF_BLOCK) % FF_DOWN_BLOCK == 0
    assert FF_DOWN_BLOCK % FF_BLOCK == 0 and wo.shape == (oa.shape[-1] + ob.shape[-1], d_model)
    const = lambda *shape: pl.BlockSpec(shape, lambda i: (0,) * len(shape),
                                        pipeline_mode=pl.Buffered(1))
    tok = lambda width: pl.BlockSpec(
        (1, tile, width), lambda i: (i // tiles_per_seq, i % tiles_per_seq, 0))
    return pl.pallas_call(
        functools.partial(_ffn_kernel, tiles_per_seq=tiles_per_seq),
        grid=(bsz * tiles_per_seq,),
        in_specs=[tok(d_model), tok(oa.shape[-1]), tok(ob.shape[-1]), const(*gb.shape),
                  const(*wo.shape), const(*gpm.shape), const(*gpf.shape), const(*wup.shape),
                  const(*cw.shape), const(*cb.shape), const(*wdn.shape), const(*gpo.shape)],
        out_specs=tok(d_model),
        out_shape=jax.ShapeDtypeStruct(x.shape, F32),
        scratch_shapes=[pltpu.VMEM((tile, d_model), BF16),
                        pltpu.VMEM((FF_GROUP, 2, tile + SUBLANES, ff_block), F32),
                        pltpu.VMEM((FF_GROUP * ff_block // FF_DOWN_BLOCK, tile, FF_DOWN_BLOCK), BF16),
                        pltpu.VMEM((2, n_up, SUBLANES, ff_block), F32),
                        pltpu.VMEM((tile, d_model), F32)],
        compiler_params=pltpu.CompilerParams(
            dimension_semantics=("arbitrary",), vmem_limit_bytes=VMEM_LIMIT_BYTES),
        name="ffn",
    )(x, oa, ob, gb, wo, gpm, gpf, wup, cw, cb, wdn, gpo)


def _rope_tables(seq):
    half = ROT_DIM // 2
    inv = ROPE_THETA ** (-jnp.arange(0, ROT_DIM, 2, dtype=F32) / ROT_DIM)
    ang = jnp.arange(seq, dtype=F32)[:, None] * inv[None, :]
    cos, sin = jnp.cos(ang), jnp.sin(ang)
    ones = jnp.ones((seq, HEAD_DIM - ROT_DIM), F32)
    zeros = jnp.zeros((seq, HEAD_DIM - ROT_DIM), F32)
    zh = jnp.zeros((seq, half), F32)
    per_head = lambda parts: jnp.tile(jnp.concatenate(parts, axis=1), (1, LANES // HEAD_DIM))
    return (per_head([cos, cos, ones]), per_head([-sin, zh, zeros]), per_head([zh, sin, zeros]))


def kernel(x, pre_mix_norm, w_in, v_norm_g, v_norm_b, w_spatial, b_spatial, out_norm_a, out_norm_b, w_out, post_mix_norm, pre_ffn_norm, w_up, conv_w, conv_b, w_down, post_ffn_norm):
    depth = w_in.shape[0]
    seq = x.shape[1]
    cos_t, sa_t, sb_t = _rope_tables(seq)
    row = lambda p: p[None, :]
    for l in range(depth):
        oa, q, k, v = _in_proj(x, row(pre_mix_norm[l]), w_in[l].astype(BF16), row(v_norm_g[l]),
                               row(v_norm_b[l]), w_spatial[l], b_spatial[l].T,
                               row(out_norm_a[l]), cos_t, sa_t, sb_t)
        ob = _attn(q, k, v)
        x = _ffn(x, oa, ob, row(out_norm_b[l]), w_out[l].astype(BF16), row(post_mix_norm[l]),
                 row(pre_ffn_norm[l]), w_up[l].astype(BF16), conv_w[l], row(conv_b[l]),
                 w_down[l].astype(BF16), row(post_ffn_norm[l]))
    return x
```

```python
import functools

import numpy as np
import jax
import jax.numpy as jnp
from jax import lax
from jax.experimental import pallas as pl
from jax.experimental.pallas import tpu as pltpu

F32 = jnp.float32
BF16 = jnp.bfloat16

A_GROUPS = 4
CHUNK = 128
HEAD_DIM = 64
ROT_DIM = 16
ROPE_THETA = 500000.0
DILATIONS = (1, 4, 16)
BAND = 128
CONV_WIDTH = 3
EPS = 1e-6
NEG_INF = -1e30

LANES = 128
SUBLANES = 8
VMEM_LIMIT_BYTES = 56 * 1024 * 1024

IN_TILE = 512
FFN_TILE = 512
FF_BLOCK = 256
FF_DOWN_BLOCK = 512
FF_GROUP = 8
ATTN_UNROLL = 16


def _rms(x, g):
    return x * lax.rsqrt(jnp.mean(x * x, axis=-1, keepdims=True) + EPS) * g


def _in_proj_kernel(x_ref, g_ref, w_ref, vg_ref, vb_ref, ws_ref, bs_ref, ga_ref,
                    cos_ref, sa_ref, sb_ref, oa_ref, q_ref, k_ref, v_ref):
    a_width = oa_ref.shape[-1]
    b_width = q_ref.shape[-1]
    tile = x_ref.shape[1]

    h = _rms(x_ref[0], g_ref[...]).astype(BF16)
    proj = jnp.dot(h, w_ref[...], preferred_element_type=F32)

    za = proj[:, :2 * a_width]
    za = 0.5 * za * (1.0 + lax.erf(za * np.float32(np.sqrt(0.5))))
    u, va = za[:, :a_width], za[:, a_width:]
    mu = jnp.mean(va, axis=-1, keepdims=True)
    vc = va - mu
    va = vc * lax.rsqrt(jnp.mean(vc * vc, axis=-1, keepdims=True) + EPS) * vg_ref[...] + vb_ref[...]
    va = va.astype(BF16)

    row = lax.broadcasted_iota(jnp.int32, (CHUNK, CHUNK), 0)
    col = lax.broadcasted_iota(jnp.int32, (CHUNK, CHUNK), 1)
    causal = col <= row
    cols = []
    for g in range(A_GROUPS):
        wsg = jnp.where(causal, ws_ref[g], 0.0).astype(BF16)
        bias = bs_ref[:, g:g + 1]
        lo = g * CHUNK
        rows = []
        for n in range(tile // CHUNK):
            vblk = va[n * CHUNK:(n + 1) * CHUNK, lo:lo + CHUNK]
            rows.append(jnp.dot(wsg, vblk, preferred_element_type=F32) + bias)
        cols.append(jnp.concatenate(rows, axis=0))
    o_a = u * jnp.concatenate(cols, axis=1)
    oa_ref[0] = _rms(o_a, ga_ref[...]).astype(oa_ref.dtype)

    cos, sa, sb = cos_ref[...], sa_ref[...], sb_ref[...]
    half = ROT_DIM // 2

    def rope(t):
        out = []
        for c in range(b_width // LANES):
            tc = t[:, c * LANES:(c + 1) * LANES]
            out.append(tc * cos + pltpu.roll(tc, LANES - half, axis=1) * sa
                       + pltpu.roll(tc, half, axis=1) * sb)
        return jnp.concatenate(out, axis=1)

    base = 2 * a_width
    q_ref[0] = rope(proj[:, base:base + b_width]) * np.float32(HEAD_DIM ** -0.5 * np.log2(np.e))
    k_ref[0] = rope(proj[:, base + b_width:base + 2 * b_width])
    v_ref[0] = proj[:, base + 2 * b_width:]


def _in_proj(x, g, w, vg, vb, ws, bs_t, ga, cos_t, sa_t, sb_t):
    bsz, seq, d_model = x.shape
    a_width = vg.shape[-1]
    b_width = (w.shape[1] - 2 * a_width) // 3
    tile = IN_TILE
    assert seq % tile == 0 and tile % CHUNK == 0 and a_width == A_GROUPS * CHUNK
    assert w.shape == (d_model, 2 * a_width + 3 * b_width) and b_width % LANES == 0
    const = lambda *shape: pl.BlockSpec(shape, lambda b, t: (0,) * len(shape),
                                        pipeline_mode=pl.Buffered(1))
    tok = lambda width: pl.BlockSpec((1, tile, width), lambda b, t: (b, t, 0))
    pos = pl.BlockSpec((tile, LANES), lambda b, t: (t, 0))
    return pl.pallas_call(
        _in_proj_kernel,
        grid=(bsz, seq // tile),
        in_specs=[tok(d_model), const(1, d_model), const(*w.shape), const(1, a_width),
                  const(1, a_width), const(*ws.shape), const(*bs_t.shape), const(1, a_width),
                  pos, pos, pos],
        out_specs=[tok(a_width), tok(b_width), tok(b_width), tok(b_width)],
        out_shape=[jax.ShapeDtypeStruct((bsz, seq, a_width), BF16),
                   jax.ShapeDtypeStruct((bsz, seq, b_width), F32),
                   jax.ShapeDtypeStruct((bsz, seq, b_width), F32),
                   jax.ShapeDtypeStruct((bsz, seq, b_width), F32)],
        compiler_params=pltpu.CompilerParams(
            dimension_semantics=("parallel", "parallel"), vmem_limit_bytes=VMEM_LIMIT_BYTES),
        name="in_proj",
    )(x, g, w, vg, vb, ws, bs_t, ga, cos_t, sa_t, sb_t)


def _attn_kernel(q_ref, k_ref, v_ref, o_ref, acc_ref, m_ref, l_ref, bias_ref):
    seq = q_ref.shape[1]
    lane = lax.broadcasted_iota(jnp.int32, (BAND, LANES), 1)
    head0 = lane < HEAD_DIM
    qi = lax.broadcasted_iota(jnp.int32, (BAND, 2 * BAND), 0)
    kj = lax.broadcasted_iota(jnp.int32, (BAND, 2 * BAND), 1)
    dist = qi + BAND - kj
    bias_ref[...] = jnp.where((dist >= 0) & (dist <= BAND), 0.0, NEG_INF).astype(F32)
    ones = jnp.ones((BAND, LANES), BF16)

    def tile(rows_q, rows_prev):
        qb = q_ref[0, rows_q, :].astype(BF16)
        zero = jnp.zeros_like(qb)
        q2 = jnp.concatenate([jnp.where(head0, qb, zero), jnp.where(head0, zero, qb)], axis=0)
        if rows_prev is None:
            kk = k_ref[0, rows_q, :].astype(BF16)
            vv = jnp.concatenate([v_ref[0, rows_q, :].astype(BF16), ones], axis=1)
            bias = bias_ref[:, BAND:]
        else:
            kk = jnp.concatenate([k_ref[0, rows_prev, :], k_ref[0, rows_q, :]], axis=0).astype(BF16)
            vv = jnp.concatenate([v_ref[0, rows_prev, :], v_ref[0, rows_q, :]], axis=0).astype(BF16)
            vv = jnp.concatenate([vv, jnp.concatenate([ones, ones], axis=0)], axis=1)
            bias = bias_ref[...]
        s = lax.dot_general(q2, kk, (((1,), (1,)), ((), ())), preferred_element_type=F32)
        yield
        s = s + jnp.concatenate([bias, bias], axis=0)
        m = jnp.max(s, axis=-1, keepdims=True)
        p = jnp.exp2(s - m)
        yield
        o2 = jnp.dot(p.astype(BF16), vv, preferred_element_type=F32)
        yield
        acc = jnp.where(head0, o2[:BAND, :LANES], o2[BAND:, :LANES])
        lf = jnp.where(head0, o2[:BAND, LANES:], o2[BAND:, LANES:])
        mf = jnp.where(head0, jnp.broadcast_to(m[:BAND], (BAND, LANES)),
                       jnp.broadcast_to(m[BAND:], (BAND, LANES)))
        return acc, mf, lf

    def run_tiles(n_tiles, tile_fn, start=0, cap=ATTN_UNROLL, staged=True):
        unroll = max(u for u in range(1, cap + 1) if (n_tiles - start) % u == 0)

        def in_step(tiles):
            running = [tile_fn(t) for t in tiles]
            while running:
                if staged:
                    running = [g for g in running if next(g, StopIteration) is not StopIteration]
                else:
                    for _ in running.pop(0):
                        pass

        if n_tiles - start == unroll:
            in_step([start + u for u in range(unroll)])
            return

        def body(i, carry):
            in_step([start + i * unroll + u for u in range(unroll)])
            return carry

        lax.fori_loop(0, (n_tiles - start) // unroll, body, 0)

    for slot, d in enumerate(DILATIONS[1:]):
        n_blk = seq // (d * BAND)

        def dilated(start, first, slot=slot, d=d):
            rows_q = pl.ds(start, BAND, stride=d)
            rows_prev = None if first else pl.ds(start - BAND * d, BAND, stride=d)
            acc, mf, lf = yield from tile(rows_q, rows_prev)
            acc_ref[slot, rows_q, :] = acc
            m_ref[slot, rows_q, :] = mf
            l_ref[slot, rows_q, :] = lf

        run_tiles(d, lambda t, dilated=dilated: dilated(t, True))
        run_tiles(d * (n_blk - 1),
                  lambda t, dilated=dilated, d=d: dilated((1 + t // d) * (BAND * d) + t % d, False),
                  staged=d <= SUBLANES)

    def merge(n):
        static = isinstance(n, int)
        start = n * BAND if static else pl.multiple_of(n * BAND, BAND)
        rows = pl.ds(start, BAND)
        rows_prev = None if (static and n == 0) else pl.ds(start - BAND, BAND)
        acc1, m1, l1 = yield from tile(rows, rows_prev)
        m2, m3 = m_ref[0, rows, :], m_ref[1, rows, :]
        m = jnp.maximum(jnp.maximum(m1, m2), m3)
        w1, w2, w3 = jnp.exp2(m1 - m), jnp.exp2(m2 - m), jnp.exp2(m3 - m)
        num = w1 * acc1 + w2 * acc_ref[0, rows, :] + w3 * acc_ref[1, rows, :]
        den = w1 * l1 + w2 * l_ref[0, rows, :] + w3 * l_ref[1, rows, :]
        o_ref[0, rows, :] = num / den

    run_tiles(ATTN_UNROLL, merge)
    run_tiles(seq // BAND, merge, start=ATTN_UNROLL)


def _attn(q, k, v):
    bsz, seq, b_width = q.shape
    assert seq % (max(DILATIONS) * BAND) == 0 and b_width % LANES == 0
    assert LANES == 2 * HEAD_DIM and DILATIONS[0] == 1
    spec = pl.BlockSpec((1, seq, LANES), lambda b, c: (b, 0, c))
    n_dilated = len(DILATIONS) - 1
    return pl.pallas_call(
        _attn_kernel,
        grid=(bsz, b_width // LANES),
        in_specs=[spec, spec, spec],
        out_specs=spec,
        out_shape=jax.ShapeDtypeStruct((bsz, seq, b_width), F32),
        scratch_shapes=[pltpu.VMEM((n_dilated, seq, LANES), F32)] * 3
        + [pltpu.VMEM((BAND, 2 * BAND), F32)],
        compiler_params=pltpu.CompilerParams(
            dimension_semantics=("parallel", "parallel"), vmem_limit_bytes=VMEM_LIMIT_BYTES),
        name="attn",
    )(q, k, v)


def _ffn_kernel(x_ref, oa_ref, ob_ref, gb_ref, wo_ref, gpm_ref, gpf_ref, wup_ref, cw_ref,
                cb_ref, wdn_ref, gpo_ref, out_ref, h_ref, up_ref, act_ref, carry_ref, acc_ref,
                *, tiles_per_seq):
    tile = x_ref.shape[1]
    d_ff = wdn_ref.shape[0]
    ff_block, n_up = FF_BLOCK, d_ff // FF_BLOCK
    per_down = FF_DOWN_BLOCK // ff_block

    def cols(part, j):
        return pl.ds(pl.multiple_of(part * d_ff + j * ff_block, ff_block), ff_block)
    first_tile = pl.program_id(0) % tiles_per_seq == 0

    mixed = jnp.concatenate([oa_ref[0], _rms(ob_ref[0], gb_ref[...]).astype(BF16)], axis=1)
    half = tile // 2
    for r0 in (0, half):
        y = jnp.dot(mixed[r0:r0 + half], wo_ref[...], preferred_element_type=F32)
        x1 = x_ref[0, r0:r0 + half] + _rms(y, gpm_ref[...])
        out_ref[0, r0:r0 + half] = x1
        h_ref[r0:r0 + half] = _rms(x1, gpf_ref[...]).astype(BF16)
    acc_ref[...] = jnp.zeros_like(acc_ref)

    def up_proj(j, sub):
        h = h_ref[...]
        for part in range(2):
            up = jnp.dot(h, wup_ref[:, cols(part, j)], preferred_element_type=F32)
            buf = up_ref.at[sub, part]
            prev = carry_ref[part, j]
            buf[pl.ds(0, SUBLANES), :] = jnp.where(first_tile, jnp.zeros_like(prev), prev)
            buf[pl.ds(SUBLANES, tile), :] = up
            carry_ref[part, j] = up[tile - SUBLANES:]

    def activate(j, sub):
        conv = []
        for part in range(2):
            buf = up_ref.at[sub, part]
            w = cw_ref[:, cols(part, j)]
            out = cb_ref[:, cols(part, j)]
            for i in range(CONV_WIDTH):
                shift = SUBLANES - (CONV_WIDTH - 1) + i
                out = out + w[i:i + 1] * buf[pl.ds(shift, tile), :]
            conv.append(out)
        gate, val = conv
        c = np.sqrt(2.0 / np.pi)
        cdf = 0.5 * (1.0 + jnp.tanh(gate * (np.float32(c) + np.float32(c * 0.044715) * (gate * gate))))
        lo = (sub % per_down) * ff_block
        act_ref[sub // per_down, :, lo:lo + ff_block] = (gate * cdf * val).astype(BF16)

    def body(g, carry):
        for sub in range(FF_GROUP):
            up_proj(g * FF_GROUP + sub, sub)
        for sub in range(FF_GROUP):
            activate(g * FF_GROUP + sub, sub)
        for p in range(FF_GROUP // per_down):
            rows = pl.ds(pl.multiple_of((g * (FF_GROUP // per_down) + p) * FF_DOWN_BLOCK,
                                        FF_DOWN_BLOCK), FF_DOWN_BLOCK)
            acc_ref[...] += jnp.dot(act_ref[p], wdn_ref[rows, :], preferred_element_type=F32)
        return carry

    lax.fori_loop(0, n_up // FF_GROUP, body, 0)
    out_ref[0] = out_ref[0] + _rms(acc_ref[...], gpo_ref[...])


def _ffn(x, oa, ob, gb, wo, gpm, gpf, wup, cw, cb, wdn, gpo):
    bsz, seq, d_model = x.shape
    tile = FFN_TILE
    tiles_per_seq = seq // tile
    ff_block, n_up = FF_BLOCK, wdn.shape[0] // FF_BLOCK
    d_ff = wdn.shape[0]
    assert seq % tile == 0 and wup.shape == (d_model, 2 * d_ff) and cw.shape == (CONV_WIDTH, 2 * d_ff)
    assert d_ff % (FF_GROUP * FF_BLOCK) == 0 and (FF_GROUP * FF_BLOCK) % FF_DOWN_BLOCK == 0
    assert FF_DOWN_BLOCK % FF_BLOCK == 0 and wo.shape == (oa.shape[-1] + ob.shape[-1], d_model)
    const = lambda *shape: pl.BlockSpec(shape, lambda i: (0,) * len(shape),
                                        pipeline_mode=pl.Buffered(1))
    tok = lambda width: pl.BlockSpec(
        (1, tile, width), lambda i: (i // tiles_per_seq, i % tiles_per_seq, 0))
    return pl.pallas_call(
        functools.partial(_ffn_kernel, tiles_per_seq=tiles_per_seq),
        grid=(bsz * tiles_per_seq,),
        in_specs=[tok(d_model), tok(oa.shape[-1]), tok(ob.shape[-1]), const(*gb.shape),
                  const(*wo.shape), const(*gpm.shape), const(*gpf.shape), const(*wup.shape),
                  const(*cw.shape), const(*cb.shape), const(*wdn.shape), const(*gpo.shape)],
        out_specs=tok(d_model),
        out_shape=jax.ShapeDtypeStruct(x.shape, F32),
        scratch_shapes=[pltpu.VMEM((tile, d_model), BF16),
                        pltpu.VMEM((FF_GROUP, 2, tile + SUBLANES, ff_block), F32),
                        pltpu.VMEM((FF_GROUP * ff_block // FF_DOWN_BLOCK, tile, FF_DOWN_BLOCK), BF16),
                        pltpu.VMEM((2, n_up, SUBLANES, ff_block), F32),
                        pltpu.VMEM((tile, d_model), F32)],
        compiler_params=pltpu.CompilerParams(
            dimension_semantics=("arbitrary",), vmem_limit_bytes=VMEM_LIMIT_BYTES),
        name="ffn",
    )(x, oa, ob, gb, wo, gpm, gpf, wup, cw, cb, wdn, gpo)


def _rope_tables(seq):
    half = ROT_DIM // 2
    inv = ROPE_THETA ** (-jnp.arange(0, ROT_DIM, 2, dtype=F32) / ROT_DIM)
    ang = jnp.arange(seq, dtype=F32)[:, None] * inv[None, :]
    cos, sin = jnp.cos(ang), jnp.sin(ang)
    ones = jnp.ones((seq, HEAD_DIM - ROT_DIM), F32)
    zeros = jnp.zeros((seq, HEAD_DIM - ROT_DIM), F32)
    zh = jnp.zeros((seq, half), F32)
    per_head = lambda parts: jnp.tile(jnp.concatenate(parts, axis=1), (1, LANES // HEAD_DIM))
    return (per_head([cos, cos, ones]), per_head([-sin, zh, zeros]), per_head([zh, sin, zeros]))


def kernel(x, pre_mix_norm, w_in, v_norm_g, v_norm_b, w_spatial, b_spatial, out_norm_a, out_norm_b, w_out, post_mix_norm, pre_ffn_norm, w_up, conv_w, conv_b, w_down, post_ffn_norm):
    depth = w_in.shape[0]
    seq = x.shape[1]
    cos_t, sa_t, sb_t = _rope_tables(seq)
    row = lambda p: p[None, :]
    for l in range(depth):
        oa, q, k, v = _in_proj(x, row(pre_mix_norm[l]), w_in[l].astype(BF16), row(v_norm_g[l]),
                               row(v_norm_b[l]), w_spatial[l], b_spatial[l].T,
                               row(out_norm_a[l]), cos_t, sa_t, sb_t)
        ob = _attn(q, k, v)
        x = _ffn(x, oa, ob, row(out_norm_b[l]), w_out[l].astype(BF16), row(post_mix_norm[l]),
                 row(pre_ffn_norm[l]), w_up[l].astype(BF16), conv_w[l], row(conv_b[l]),
                 w_down[l].astype(BF16), row(post_ffn_norm[l]))
    return x
```

```python
import functools

import numpy as np
import jax
import jax.numpy as jnp
from jax import lax
from jax.experimental import pallas as pl
from jax.experimental.pallas import tpu as pltpu

F32 = jnp.float32
BF16 = jnp.bfloat16

A_GROUPS = 4
CHUNK = 128
HEAD_DIM = 64
ROT_DIM = 16
ROPE_THETA = 500000.0
DILATIONS = (1, 4, 16)
BAND = 128
CONV_WIDTH = 3
EPS = 1e-6
NEG_INF = -1e30

LANES = 128
SUBLANES = 8
VMEM_LIMIT_BYTES = 56 * 1024 * 1024

IN_TILE = 512
FFN_TILE = 512
FF_BLOCK = 256
FF_DOWN_BLOCK = 512
FF_GROUP = 8
ATTN_UNROLL = 16


def _rms(x, g):
    return x * lax.rsqrt(jnp.mean(x * x, axis=-1, keepdims=True) + EPS) * g


def _in_proj_kernel(x_ref, g_ref, w_ref, vg_ref, vb_ref, ws_ref, bs_ref, ga_ref,
                    cos_ref, sa_ref, sb_ref, oa_ref, q_ref, k_ref, v_ref):
    a_width = oa_ref.shape[-1]
    b_width = q_ref.shape[-1]
    tile = x_ref.shape[1]

    h = _rms(x_ref[0], g_ref[...]).astype(BF16)
    proj = jnp.dot(h, w_ref[...], preferred_element_type=F32)

    za = proj[:, :2 * a_width]
    za = 0.5 * za * (1.0 + lax.erf(za * np.float32(np.sqrt(0.5))))
    u, va = za[:, :a_width], za[:, a_width:]
    mu = jnp.mean(va, axis=-1, keepdims=True)
    vc = va - mu
    va = vc * lax.rsqrt(jnp.mean(vc * vc, axis=-1, keepdims=True) + EPS) * vg_ref[...] + vb_ref[...]
    va = va.astype(BF16)

    row = lax.broadcasted_iota(jnp.int32, (CHUNK, CHUNK), 0)
    col = lax.broadcasted_iota(jnp.int32, (CHUNK, CHUNK), 1)
    causal = col <= row
    cols = []
    for g in range(A_GROUPS):
        wsg = jnp.where(causal, ws_ref[g], 0.0).astype(BF16)
        bias = bs_ref[:, g:g + 1]
        lo = g * CHUNK
        rows = []
        for n in range(tile // CHUNK):
            vblk = va[n * CHUNK:(n + 1) * CHUNK, lo:lo + CHUNK]
            rows.append(jnp.dot(wsg, vblk, preferred_element_type=F32) + bias)
        cols.append(jnp.concatenate(rows, axis=0))
    o_a = u * jnp.concatenate(cols, axis=1)
    oa_ref[0] = _rms(o_a, ga_ref[...]).astype(oa_ref.dtype)

    cos, sa, sb = cos_ref[...], sa_ref[...], sb_ref[...]
    half = ROT_DIM // 2

    def rope(t):
        out = []
        for c in range(b_width // LANES):
            tc = t[:, c * LANES:(c + 1) * LANES]
            out.append(tc * cos + pltpu.roll(tc, LANES - half, axis=1) * sa
                       + pltpu.roll(tc, half, axis=1) * sb)
        return jnp.concatenate(out, axis=1)

    base = 2 * a_width
    q_ref[0] = rope(proj[:, base:base + b_width]) * np.float32(HEAD_DIM ** -0.5 * np.log2(np.e))
    k_ref[0] = rope(proj[:, base + b_width:base + 2 * b_width])
    v_ref[0] = proj[:, base + 2 * b_width:]


def _in_proj(x, g, w, vg, vb, ws, bs_t, ga, cos_t, sa_t, sb_t):
    bsz, seq, d_model = x.shape
    a_width = vg.shape[-1]
    b_width = (w.shape[1] - 2 * a_width) // 3
    tile = IN_TILE
    assert seq % tile == 0 and tile % CHUNK == 0 and a_width == A_GROUPS * CHUNK
    assert w.shape == (d_model, 2 * a_width + 3 * b_width) and b_width % LANES == 0
    const = lambda *shape: pl.BlockSpec(shape, lambda b, t: (0,) * len(shape),
                                        pipeline_mode=pl.Buffered(1))
    tok = lambda width: pl.BlockSpec((1, tile, width), lambda b, t: (b, t, 0))
    pos = pl.BlockSpec((tile, LANES), lambda b, t: (t, 0))
    return pl.pallas_call(
        _in_proj_kernel,
        grid=(bsz, seq // tile),
        in_specs=[tok(d_model), const(1, d_model), const(*w.shape), const(1, a_width),
                  const(1, a_width), const(*ws.shape), const(*bs_t.shape), const(1, a_width),
                  pos, pos, pos],
        out_specs=[tok(a_width), tok(b_width), tok(b_width), tok(b_width)],
        out_shape=[jax.ShapeDtypeStruct((bsz, seq, a_width), BF16),
                   jax.ShapeDtypeStruct((bsz, seq, b_width), F32),
                   jax.ShapeDtypeStruct((bsz, seq, b_width), F32),
                   jax.ShapeDtypeStruct((bsz, seq, b_width), F32)],
        compiler_params=pltpu.CompilerParams(
            dimension_semantics=("parallel", "parallel"), vmem_limit_bytes=VMEM_LIMIT_BYTES),
        name="in_proj",
    )(x, g, w, vg, vb, ws, bs_t, ga, cos_t, sa_t, sb_t)


def _attn_kernel(q_ref, k_ref, v_ref, o_ref, acc_ref, m_ref, l_ref, bias_ref):
    seq = q_ref.shape[1]
    lane = lax.broadcasted_iota(jnp.int32, (BAND, LANES), 1)
    head0 = lane < HEAD_DIM
    qi = lax.broadcasted_iota(jnp.int32, (BAND, 2 * BAND), 0)
    kj = lax.broadcasted_iota(jnp.int32, (BAND, 2 * BAND), 1)
    dist = qi + BAND - kj
    bias_ref[...] = jnp.where((dist >= 0) & (dist <= BAND), 0.0, NEG_INF).astype(F32)
    ones = jnp.ones((BAND, LANES), BF16)

    def tile(rows_q, rows_prev):
        qb = q_ref[0, rows_q, :].astype(BF16)
        zero = jnp.zeros_like(qb)
        q2 = jnp.concatenate([jnp.where(head0, qb, zero), jnp.where(head0, zero, qb)], axis=0)
        if rows_prev is None:
            kk = k_ref[0, rows_q, :].astype(BF16)
            vv = jnp.concatenate([v_ref[0, rows_q, :].astype(BF16), ones], axis=1)
            bias = bias_ref[:, BAND:]
        else:
            kk = jnp.concatenate([k_ref[0, rows_prev, :], k_ref[0, rows_q, :]], axis=0).astype(BF16)
            vv = jnp.concatenate([v_ref[0, rows_prev, :], v_ref[0, rows_q, :]], axis=0).astype(BF16)
            vv = jnp.concatenate([vv, jnp.concatenate([ones, ones], axis=0)], axis=1)
            bias = bias_ref[...]
        s = lax.dot_general(q2, kk, (((1,), (1,)), ((), ())), preferred_element_type=F32)
        yield
        s = s + jnp.concatenate([bias, bias], axis=0)
        m = jnp.max(s, axis=-1, keepdims=True)
        p = jnp.exp2(s - m)
        yield
        o2 = jnp.dot(p.astype(BF16), vv, preferred_element_type=F32)
        yield
        acc = jnp.where(head0, o2[:BAND, :LANES], o2[BAND:, :LANES])
        lf = jnp.where(head0, o2[:BAND, LANES:], o2[BAND:, LANES:])
        mf = jnp.where(head0, jnp.broadcast_to(m[:BAND], (BAND, LANES)),
                       jnp.broadcast_to(m[BAND:], (BAND, LANES)))
        return acc, mf, lf

    def run_tiles(n_tiles, tile_fn, start=0, cap=ATTN_UNROLL, staged=True):
        unroll = max(u for u in range(1, cap + 1) if (n_tiles - start) % u == 0)

        def in_step(tiles):
            running = [tile_fn(t) for t in tiles]
            while running:
                if staged:
                    running = [g for g in running if next(g, StopIteration) is not StopIteration]
                else:
                    for _ in running.pop(0):
                        pass

        if n_tiles - start == unroll:
            in_step([start + u for u in range(unroll)])
            return

        def body(i, carry):
            in_step([start + i * unroll + u for u in range(unroll)])
            return carry

        lax.fori_loop(0, (n_tiles - start) // unroll, body, 0)

    for slot, d in enumerate(DILATIONS[1:]):
        n_blk = seq // (d * BAND)

        def dilated(start, first, slot=slot, d=d):
            rows_q = pl.ds(start, BAND, stride=d)
            rows_prev = None if first else pl.ds(start - BAND * d, BAND, stride=d)
            acc, mf, lf = yield from tile(rows_q, rows_prev)
            if d > SUBLANES:
                acc_ref[slot, rows_q, :] = acc / lf
                m_ref[slot, rows_q, :] = mf + jnp.log2(lf)
            else:
                acc_ref[slot, rows_q, :] = acc
                m_ref[slot, rows_q, :] = mf
                l_ref[slot, rows_q, :] = lf

        run_tiles(d, lambda t, dilated=dilated: dilated(t, True))
        run_tiles(d * (n_blk - 1),
                  lambda t, dilated=dilated, d=d: dilated((1 + t // d) * (BAND * d) + t % d, False),
                  staged=d <= SUBLANES)

    def merge(n):
        static = isinstance(n, int)
        start = n * BAND if static else pl.multiple_of(n * BAND, BAND)
        rows = pl.ds(start, BAND)
        rows_prev = None if (static and n == 0) else pl.ds(start - BAND, BAND)
        acc1, m1, l1 = yield from tile(rows, rows_prev)
        m_p = [m_ref[slot, rows, :] for slot in range(len(DILATIONS) - 1)]
        m = functools.reduce(jnp.maximum, m_p, m1)
        w1 = jnp.exp2(m1 - m)
        num, den = w1 * acc1, w1 * l1
        for slot, d in enumerate(DILATIONS[1:]):
            w = jnp.exp2(m_p[slot] - m)
            num = num + w * acc_ref[slot, rows, :]
            den = den + (w if d > SUBLANES else w * l_ref[slot, rows, :])
        o_ref[0, rows, :] = num / den

    run_tiles(ATTN_UNROLL, merge)
    run_tiles(seq // BAND, merge, start=ATTN_UNROLL)


def _attn(q, k, v):
    bsz, seq, b_width = q.shape
    assert seq % (max(DILATIONS) * BAND) == 0 and b_width % LANES == 0
    assert LANES == 2 * HEAD_DIM and DILATIONS[0] == 1
    spec = pl.BlockSpec((1, seq, LANES), lambda b, c: (b, 0, c))
    n_dilated = len(DILATIONS) - 1
    return pl.pallas_call(
        _attn_kernel,
        grid=(bsz, b_width // LANES),
        in_specs=[spec, spec, spec],
        out_specs=spec,
        out_shape=jax.ShapeDtypeStruct((bsz, seq, b_width), F32),
        scratch_shapes=[pltpu.VMEM((n_dilated, seq, LANES), F32)] * 3
        + [pltpu.VMEM((BAND, 2 * BAND), F32)],
        compiler_params=pltpu.CompilerParams(
            dimension_semantics=("parallel", "parallel"), vmem_limit_bytes=VMEM_LIMIT_BYTES),
        name="attn",
    )(q, k, v)


def _ffn_kernel(x_ref, oa_ref, ob_ref, gb_ref, wo_ref, gpm_ref, gpf_ref, wup_ref, cw_ref,
                cb_ref, wdn_ref, gpo_ref, out_ref, h_ref, up_ref, act_ref, carry_ref, acc_ref,
                *, tiles_per_seq):
    tile = x_ref.shape[1]
    d_ff = wdn_ref.shape[0]
    ff_block, n_up = FF_BLOCK, d_ff // FF_BLOCK
    per_down = FF_DOWN_BLOCK // ff_block

    def cols(part, j):
        return pl.ds(pl.multiple_of(part * d_ff + j * ff_block, ff_block), ff_block)
    first_tile = pl.program_id(0) % tiles_per_seq == 0

    mixed = jnp.concatenate([oa_ref[0], _rms(ob_ref[0], gb_ref[...]).astype(BF16)], axis=1)
    half = tile // 2
    for r0 in (0, half):
        y = jnp.dot(mixed[r0:r0 + half], wo_ref[...], preferred_element_type=F32)
        x1 = x_ref[0, r0:r0 + half] + _rms(y, gpm_ref[...])
        out_ref[0, r0:r0 + half] = x1
        h_ref[r0:r0 + half] = _rms(x1, gpf_ref[...]).astype(BF16)
    acc_ref[...] = jnp.zeros_like(acc_ref)

    def up_proj(j, sub):
        h = h_ref[...]
        for part in range(2):
            up = jnp.dot(h, wup_ref[:, cols(part, j)], preferred_element_type=F32)
            buf = up_ref.at[sub, part]
            prev = carry_ref[part, j]
            buf[pl.ds(0, SUBLANES), :] = jnp.where(first_tile, jnp.zeros_like(prev), prev)
            buf[pl.ds(SUBLANES, tile), :] = up
            carry_ref[part, j] = up[tile - SUBLANES:]

    def activate(j, sub):
        conv = []
        for part in range(2):
            buf = up_ref.at[sub, part]
            w = cw_ref[:, cols(part, j)]
            out = cb_ref[:, cols(part, j)]
            for i in range(CONV_WIDTH):
                shift = SUBLANES - (CONV_WIDTH - 1) + i
                out = out + w[i:i + 1] * buf[pl.ds(shift, tile), :]
            conv.append(out)
        gate, val = conv
        c = np.sqrt(2.0 / np.pi)
        cdf = 0.5 * (1.0 + jnp.tanh(gate * (np.float32(c) + np.float32(c * 0.044715) * (gate * gate))))
        lo = (sub % per_down) * ff_block
        act_ref[sub // per_down, :, lo:lo + ff_block] = (gate * cdf * val).astype(BF16)

    def body(g, carry):
        for sub in range(FF_GROUP):
            up_proj(g * FF_GROUP + sub, sub)
        for sub in range(FF_GROUP):
            activate(g * FF_GROUP + sub, sub)
        for p in range(FF_GROUP // per_down):
            rows = pl.ds(pl.multiple_of((g * (FF_GROUP // per_down) + p) * FF_DOWN_BLOCK,
                                        FF_DOWN_BLOCK), FF_DOWN_BLOCK)
            acc_ref[...] += jnp.dot(act_ref[p], wdn_ref[rows, :], preferred_element_type=F32)
        return carry

    lax.fori_loop(0, n_up // FF_GROUP, body, 0)
    out_ref[0] = out_ref[0] + _rms(acc_ref[...], gpo_ref[...])


def _ffn(x, oa, ob, gb, wo, gpm, gpf, wup, cw, cb, wdn, gpo):
    bsz, seq, d_model = x.shape
    tile = FFN_TILE
    tiles_per_seq = seq // tile
    ff_block, n_up = FF_BLOCK, wdn.shape[0] // FF_BLOCK
    d_ff = wdn.shape[0]
    assert seq % tile == 0 and wup.shape == (d_model, 2 * d_ff) and cw.shape == (CONV_WIDTH, 2 * d_ff)
    assert d_ff % (FF_GROUP * FF_BLOCK) == 0 and (FF_GROUP * FF_BLOCK) % FF_DOWN_BLOCK == 0
    assert FF_DOWN_BLOCK % FF_BLOCK == 0 and wo.shape == (oa.shape[-1] + ob.shape[-1], d_model)
    const = lambda *shape: pl.BlockSpec(shape, lambda i: (0,) * len(shape),
                                        pipeline_mode=pl.Buffered(1))
    tok = lambda width: pl.BlockSpec(
        (1, tile, width), lambda i: (i // tiles_per_seq, i % tiles_per_seq, 0))
    return pl.pallas_call(
        functools.partial(_ffn_kernel, tiles_per_seq=tiles_per_seq),
        grid=(bsz * tiles_per_seq,),
        in_specs=[tok(d_model), tok(oa.shape[-1]), tok(ob.shape[-1]), const(*gb.shape),
                  const(*wo.shape), const(*gpm.shape), const(*gpf.shape), const(*wup.shape),
                  const(*cw.shape), const(*cb.shape), const(*wdn.shape), const(*gpo.shape)],
        out_specs=tok(d_model),
        out_shape=jax.ShapeDtypeStruct(x.shape, F32),
        scratch_shapes=[pltpu.VMEM((tile, d_model), BF16),
                        pltpu.VMEM((FF_GROUP, 2, tile + SUBLANES, ff_block), F32),
                        pltpu.VMEM((FF_GROUP * ff_block // FF_DOWN_BLOCK, tile, FF_DOWN_BLOCK), BF16),
                        pltpu.VMEM((2, n_up, SUBLANES, ff_block), F32),
                        pltpu.VMEM((tile, d_model), F32)],
        compiler_params=pltpu.CompilerParams(
            dimension_semantics=("arbitrary",), vmem_limit_bytes=VMEM_LIMIT_BYTES),
        name="ffn",
    )(x, oa, ob, gb, wo, gpm, gpf, wup, cw, cb, wdn, gpo)


def _rope_tables(seq):
    half = ROT_DIM // 2
    inv = ROPE_THETA ** (-jnp.arange(0, ROT_DIM, 2, dtype=F32) / ROT_DIM)
    ang = jnp.arange(seq, dtype=F32)[:, None] * inv[None, :]
    cos, sin = jnp.cos(ang), jnp.sin(ang)
    ones = jnp.ones((seq, HEAD_DIM - ROT_DIM), F32)
    zeros = jnp.zeros((seq, HEAD_DIM - ROT_DIM), F32)
    zh = jnp.zeros((seq, half), F32)
    per_head = lambda parts: jnp.tile(jnp.concatenate(parts, axis=1), (1, LANES // HEAD_DIM))
    return (per_head([cos, cos, ones]), per_head([-sin, zh, zeros]), per_head([zh, sin, zeros]))


def kernel(x, pre_mix_norm, w_in, v_norm_g, v_norm_b, w_spatial, b_spatial, out_norm_a, out_norm_b, w_out, post_mix_norm, pre_ffn_norm, w_up, conv_w, conv_b, w_down, post_ffn_norm):
    depth = w_in.shape[0]
    seq = x.shape[1]
    cos_t, sa_t, sb_t = _rope_tables(seq)
    row = lambda p: p[None, :]
    for l in range(depth):
        oa, q, k, v = _in_proj(x, row(pre_mix_norm[l]), w_in[l].astype(BF16), row(v_norm_g[l]),
                               row(v_norm_b[l]), w_spatial[l], b_spatial[l].T,
                               row(out_norm_a[l]), cos_t, sa_t, sb_t)
        ob = _attn(q, k, v)
        x = _ffn(x, oa, ob, row(out_norm_b[l]), w_out[l].astype(BF16), row(post_mix_norm[l]),
                 row(pre_ffn_norm[l]), w_up[l].astype(BF16), conv_w[l], row(conv_b[l]),
                 w_down[l].astype(BF16), row(post_ffn_norm[l]))
    return x
```
